```python
import math
import jax, jax.numpy as jnp
from jax import lax
import numpy as np

D_MODEL = 1024
BATCH = 32
SEQ = 2048
DEPTH = 1
DEC_BATCH = 16
DEC_SEQ = 4096
PAST_LEN = 128

HEAD_DIM = 64
N_HEADS = D_MODEL // (2 * HEAD_DIM)
ATT_WIDTH = N_HEADS * 2 * HEAD_DIM
ROT_DIM = HEAD_DIM // 4
ROPE_THETA = 500000.0
D_CONV = D_MODEL
CONV_WIDTH = 31
D_FF = 2816
FFN_CONV_WIDTH = 3
Q_BLOCK = 128
LN_EPS = 1e-5
DEEPNORM_ALPHA = (2.0 * DEPTH) ** 0.25
DEEPNORM_BETA = (8.0 * DEPTH) ** -0.25
Q_OFF = 0
K_OFF = Q_OFF + ATT_WIDTH
V_OFF = K_OFF + ATT_WIDTH
GLU_OFF = V_OFF + ATT_WIDTH
GATE_OFF = GLU_OFF + 2 * D_CONV
IN_WIDTH = GATE_OFF + 2 * D_MODEL

kernel_name = "hybrid_diffattn_conformer_encoder"


def _norm(x):
    xf = x.astype(jnp.float32)
    mu = jnp.mean(xf, -1, keepdims=True)
    var = jnp.mean(jnp.square(xf - mu), -1, keepdims=True)
    return ((xf - mu) * lax.rsqrt(var + LN_EPS)).astype(x.dtype)


def _layer_norm(x, g, b):
    return _norm(x) * g + b


def _rms_norm(x, g):
    xf = x.astype(jnp.float32)
    y = xf * lax.rsqrt(jnp.mean(jnp.square(xf), -1, keepdims=True) + LN_EPS)
    return y.astype(x.dtype) * g


def _rope_tables(seq_len):
    inv_freq = ROPE_THETA ** (-jnp.arange(0, ROT_DIM, 2, dtype=jnp.float32) / ROT_DIM)
    ang = jnp.arange(seq_len, dtype=jnp.float32)[:, None] * inv_freq[None, :]
    return jnp.cos(ang), jnp.sin(ang)


def _partial_rope(t, cos, sin):
    half = ROT_DIM // 2
    r1, r2, rest = t[..., :half], t[..., half:ROT_DIM], t[..., ROT_DIM:]
    c = cos[:, None, None, :].astype(t.dtype)
    s = sin[:, None, None, :].astype(t.dtype)
    return jnp.concatenate([r1 * c - r2 * s, r2 * c + r1 * s, rest], axis=-1)


def _diff_attention(q, k, v, lam):
    bsz, seq = q.shape[0], q.shape[1]
    n_blk = seq // Q_BLOCK
    qb = q.reshape(bsz, n_blk, Q_BLOCK, N_HEADS, 2, HEAD_DIM).transpose(1, 0, 2, 3, 4, 5)
    scale = HEAD_DIM ** -0.5

    def one_block(q_blk):
        s = jnp.einsum('bqhmd,bkhmd->bhmqk', q_blk, k).astype(jnp.float32) * scale
        p = jax.nn.softmax(s, axis=-1)
        w = p[:, :, 0] - lam * p[:, :, 1]
        return jnp.einsum('bhqk,bkhe->bqhe', w.astype(v.dtype), v)

    o = lax.map(one_block, qb)
    return o.transpose(1, 0, 2, 3, 4).reshape(bsz, seq, N_HEADS, 2 * HEAD_DIM)


def _depthwise_conv(x, w, b):
    width, ch = w.shape
    pad = (width - 1) // 2
    y = lax.conv_general_dilated(x, w[:, None, :].astype(x.dtype), window_strides=(1,),
                                 padding=[(pad, pad)], dimension_numbers=('NWC', 'WIO', 'NWC'),
                                 feature_group_count=ch)
    return y + b


def _encoder_layer(x, c, layer_idx, w_ada, b_ada, w_in, b_gate, lambda_q1, lambda_k1, lambda_q2,
                   lambda_k2, subln_g, conv_w, conv_b, conv_ln_g, conv_ln_b, w_pw2, b_pw2, w_o,
                   ln1_g, ln1_b, w_up, ffn_conv_w, ffn_conv_b, w_down, ln2_g, ln2_b):
    bsz, seq, _ = x.shape
    mod = jax.nn.silu(c) @ w_ada + b_ada
    sh1, sc1, g1, sh2, sc2, g2 = jnp.split(mod[:, None, :], 6, axis=-1)

    h = _norm(x) * (1 + sc1) + sh1
    proj = h @ w_in
    q = proj[..., Q_OFF:K_OFF].reshape(bsz, seq, N_HEADS, 2, HEAD_DIM)
    k = proj[..., K_OFF:V_OFF].reshape(bsz, seq, N_HEADS, 2, HEAD_DIM)
    v = proj[..., V_OFF:GLU_OFF].reshape(bsz, seq, N_HEADS, 2 * HEAD_DIM)
    glu = proj[..., GLU_OFF:GATE_OFF]
    gate_logits = proj[..., GATE_OFF:] + b_gate

    cos, sin = _rope_tables(seq)
    q = _partial_rope(q, cos, sin)
    k = _partial_rope(k, cos, sin)
    lam_init = 0.8 - 0.6 * math.exp(-0.3 * layer_idx)
    lam = (jnp.exp(jnp.sum(lambda_q1.astype(jnp.float32) * lambda_k1.astype(jnp.float32)))
           - jnp.exp(jnp.sum(lambda_q2.astype(jnp.float32) * lambda_k2.astype(jnp.float32))) + lam_init)
    att = _diff_attention(q, k, v, lam)
    y_a = (_rms_norm(att, subln_g) * (1.0 - lam_init)).reshape(bsz, seq, ATT_WIDTH)

    ga, gb = jnp.split(glu, 2, axis=-1)
    u = ga * jax.nn.sigmoid(gb)
    u = _depthwise_conv(u, conv_w, conv_b)
    u = jax.nn.silu(_layer_norm(u, conv_ln_g, conv_ln_b))
    y_b = u @ w_pw2 + b_pw2

    gate_a, gate_b = jnp.split(jax.nn.sigmoid(gate_logits), 2, axis=-1)
    mix = (gate_a * y_a + gate_b * y_b) @ w_o
    x = _layer_norm(DEEPNORM_ALPHA * x + g1 * mix, ln1_g, ln1_b)

    h = _norm(x) * (1 + sc2) + sh2
    up = _depthwise_conv(h @ w_up, ffn_conv_w, ffn_conv_b)
    fg, fv = jnp.split(up, 2, axis=-1)
    f = (jax.nn.gelu(fg, approximate=False) * fv) @ w_down
    x = _layer_norm(DEEPNORM_ALPHA * x + g2 * f, ln2_g, ln2_b)
    return x


def setup_inputs(seed: int = 0) -> dict:
    key = jax.random.key(seed)
    ks = jax.random.split(key, 32)
    f32 = jnp.float32
    nrm = lambda k, shape, s: jax.random.normal(k, shape, f32) * s
    d = D_MODEL
    qk_scale = d ** -0.5
    col_scale = jnp.concatenate([
        jnp.full((2 * ATT_WIDTH,), qk_scale, f32),
        jnp.full((ATT_WIDTH,), DEEPNORM_BETA * qk_scale, f32),
        jnp.full((2 * D_CONV + 2 * d,), qk_scale, f32)])
    return {
        "x_prompt": nrm(ks[0], (BATCH, SEQ, d), 1.0),
        "x_sample": nrm(ks[1], (DEC_BATCH, DEC_SEQ, d), 1.0),
        "c_prompt": nrm(ks[2], (BATCH, d), 1.0),
        "c_sample": nrm(ks[3], (DEC_BATCH, d), 1.0),
        "w_ada": nrm(ks[4], (DEPTH, d, 6 * d), 0.5 * d ** -0.5),
        "b_ada": nrm(ks[5], (DEPTH, 6 * d), 0.02),
        "w_in": jax.random.normal(ks[6], (DEPTH, d, IN_WIDTH), f32) * col_scale,
        "b_gate": nrm(ks[7], (DEPTH, 2 * d), 0.02),
        "lambda_q1": nrm(ks[8], (DEPTH, HEAD_DIM), 0.1),
        "lambda_k1": nrm(ks[9], (DEPTH, HEAD_DIM), 0.1),
        "lambda_q2": nrm(ks[10], (DEPTH, HEAD_DIM), 0.1),
        "lambda_k2": nrm(ks[11], (DEPTH, HEAD_DIM), 0.1),
        "subln_g": 1.0 + nrm(ks[12], (DEPTH, 2 * HEAD_DIM), 0.02),
        "conv_w": nrm(ks[13], (DEPTH, CONV_WIDTH, D_CONV), CONV_WIDTH ** -0.5),
        "conv_b": nrm(ks[14], (DEPTH, D_CONV), 0.02),
        "conv_ln_g": 1.0 + nrm(ks[15], (DEPTH, D_CONV), 0.02),
        "conv_ln_b": nrm(ks[16], (DEPTH, D_CONV), 0.02),
        "w_pw2": nrm(ks[17], (DEPTH, D_CONV, d), DEEPNORM_BETA * D_CONV ** -0.5),
        "b_pw2": nrm(ks[18], (DEPTH, d), 0.02),
        "w_o": nrm(ks[19], (DEPTH, d, d), DEEPNORM_BETA * d ** -0.5),
        "ln1_g": 1.0 + nrm(ks[20], (DEPTH, d), 0.02),
        "ln1_b": nrm(ks[21], (DEPTH, d), 0.02),
        "w_up": nrm(ks[22], (DEPTH, d, 2 * D_FF), DEEPNORM_BETA * d ** -0.5),
        "ffn_conv_w": nrm(ks[23], (DEPTH, FFN_CONV_WIDTH, 2 * D_FF), FFN_CONV_WIDTH ** -0.5),
        "ffn_conv_b": nrm(ks[24], (DEPTH, 2 * D_FF), 0.02),
        "w_down": nrm(ks[25], (DEPTH, D_FF, d), DEEPNORM_BETA * D_FF ** -0.5),
        "ln2_g": 1.0 + nrm(ks[26], (DEPTH, d), 0.02),
        "ln2_b": nrm(ks[27], (DEPTH, d), 0.02),
    }


def reference(x_prompt, x_sample, c_prompt, c_sample, w_ada, b_ada, w_in, b_gate, lambda_q1,
              lambda_k1, lambda_q2, lambda_k2, subln_g, conv_w, conv_b, conv_ln_g, conv_ln_b,
              w_pw2, b_pw2, w_o, ln1_g, ln1_b, w_up, ffn_conv_w, ffn_conv_b, w_down, ln2_g, ln2_b):
    y_prompt = x_prompt
    y_sample = x_sample
    for l in range(DEPTH):
        lw = (w_ada[l], b_ada[l], w_in[l], b_gate[l], lambda_q1[l], lambda_k1[l], lambda_q2[l],
              lambda_k2[l], subln_g[l], conv_w[l], conv_b[l], conv_ln_g[l], conv_ln_b[l], w_pw2[l],
              b_pw2[l], w_o[l], ln1_g[l], ln1_b[l], w_up[l], ffn_conv_w[l], ffn_conv_b[l],
              w_down[l], ln2_g[l], ln2_b[l])
        y_prompt = _encoder_layer(y_prompt, c_prompt, l, *lw)
        y_sample = _encoder_layer(y_sample, c_sample, l, *lw)
    return (y_prompt, y_sample)
```

```python
import functools
import math

import jax
import jax.numpy as jnp
from jax import lax
from jax.experimental import pallas as pl
from jax.experimental.pallas import tpu as pltpu

F32 = jnp.float32
BF16 = jnp.bfloat16

HEAD_DIM = 64
HEAD_WIDTH = 2 * HEAD_DIM
ROT_DIM = HEAD_DIM // 4
ROPE_THETA = 500000.0
LN_EPS = 1e-5
LANES = 128
VMEM_LIMIT_BYTES = 56 * 1024 * 1024


def _compiler_params(n_grid_dims):
    return pltpu.CompilerParams(
        dimension_semantics=("arbitrary",) * n_grid_dims,
        vmem_limit_bytes=VMEM_LIMIT_BYTES)


def _resident(shape):
    zeros = (0,) * len(shape)
    return pl.BlockSpec(shape, lambda *_: zeros, pipeline_mode=pl.Buffered(1))


def _tiles(seq):
    tm = min(512, seq)
    tq = max(128, min(seq, (1 << 20) // seq))
    return tm, tq


def _norm(x):
    mu = jnp.mean(x, axis=-1, keepdims=True)
    xc = x - mu
    var = jnp.mean(xc * xc, axis=-1, keepdims=True)
    return xc * lax.rsqrt(var + LN_EPS)


def _mod_kernel(c_ref, w_ref, b_ref, o_ref):
    c = c_ref[...]
    a = c * jax.nn.sigmoid(c)
    o_ref[...] = jnp.dot(a, w_ref[...], preferred_element_type=F32,
                         precision=lax.Precision.HIGHEST) + b_ref[...]


def _modulation(c, w_ada, b_ada):
    nb, d = c.shape
    n_out = w_ada.shape[1]
    return pl.pallas_call(
        _mod_kernel,
        grid=(n_out // d,),
        in_specs=[pl.BlockSpec((nb, d), lambda j: (0, 0)),
                  pl.BlockSpec((d, d), lambda j: (0, j)),
                  pl.BlockSpec((1, d), lambda j: (0, j))],
        out_specs=pl.BlockSpec((nb, d), lambda j: (0, j)),
        out_shape=jax.ShapeDtypeStruct((nb, n_out), F32),
        compiler_params=_compiler_params(1),
        name="adaln_modulation",
    )(c, w_ada, b_ada.reshape(1, n_out))


def _lambda_kernel(l_ref, o_ref, *, lam_init):
    a = jnp.sum(l_ref[0:1, :] * l_ref[1:2, :], axis=-1, keepdims=True)
    b = jnp.sum(l_ref[2:3, :] * l_ref[3:4, :], axis=-1, keepdims=True)
    lam = jnp.exp(a) - jnp.exp(b) + lam_init
    o_ref[...] = jnp.broadcast_to(lam, o_ref.shape)


def _lambda(lq1, lk1, lq2, lk2, lam_init):
    stacked = jnp.stack([lq1, lk1, lq2, lk2]).astype(F32)
    return pl.pallas_call(
        functools.partial(_lambda_kernel, lam_init=lam_init),
        out_shape=jax.ShapeDtypeStruct((1, LANES), F32),
        name="diff_lambda",
    )(stacked)


def _rope_tables(seq):
    half = ROT_DIM // 2
    inv_freq = ROPE_THETA ** (-jnp.arange(0, ROT_DIM, 2, dtype=F32) / ROT_DIM)
    ang = jnp.arange(seq, dtype=F32)[:, None] * inv_freq[None, :]
    cos, sin = jnp.cos(ang), jnp.sin(ang)
    ones = jnp.ones((seq, HEAD_DIM - ROT_DIM), F32)
    zeros = jnp.zeros((seq, HEAD_DIM - ROT_DIM), F32)
    zh = jnp.zeros((seq, half), F32)
    cos_t = jnp.concatenate([cos, cos, ones], axis=1)
    sin_a = jnp.concatenate([-sin, zh, zeros], axis=1)
    sin_b = jnp.concatenate([zh, sin, zeros], axis=1)
    rep = LANES // HEAD_DIM
    return tuple(jnp.tile(t, (1, rep)) for t in (cos_t, sin_a, sin_b))


def _inproj_kernel(x_ref, mod_ref, w_ref, bg_ref, cos_ref, sa_ref, sb_ref,
                   q_ref, k_ref, v_ref, u_ref, g_ref, *, d, qk_scale):
    half = ROT_DIM // 2
    sh1 = mod_ref[0, 0:1, :]
    sc1 = mod_ref[0, 1:2, :]
    h = (_norm(x_ref[0]) * (1.0 + sc1) + sh1).astype(BF16)

    def proj(c0, width):
        return jnp.dot(h, w_ref[:, c0:c0 + width], preferred_element_type=F32)

    cos_t, sin_a, sin_b = cos_ref[...], sa_ref[...], sb_ref[...]

    def store_rope(t, o_ref, scale):
        for j in range(d // LANES):
            s = t[:, j * LANES:(j + 1) * LANES]
            r = (s * cos_t + pltpu.roll(s, LANES - half, 1) * sin_a
                 + pltpu.roll(s, half, 1) * sin_b)
            if scale != 1.0:
                r = r * scale
            o_ref[0, :, j * LANES:(j + 1) * LANES] = r.astype(BF16)

    store_rope(proj(0, d), q_ref, qk_scale)
    store_rope(proj(d, d), k_ref, 1.0)
    v_ref[0] = proj(2 * d, d).astype(BF16)
    glu_a = proj(3 * d, d)
    glu_b = proj(4 * d, d)
    u_ref[0] = (glu_a * jax.nn.sigmoid(glu_b)).astype(BF16)
    gate = proj(5 * d, 2 * d) + bg_ref[...]
    g_ref[0] = jax.nn.sigmoid(gate).astype(BF16)


def _in_projection(x, mod3, w_in, b_gate, tables, tm):
    nb, seq, d = x.shape
    n_in = w_in.shape[1]
    tok = lambda b, i: (b, i, 0)
    tab = lambda b, i: (i, 0)
    act = lambda width: jax.ShapeDtypeStruct((nb, seq, width), BF16)
    return pl.pallas_call(
        functools.partial(_inproj_kernel, d=d, qk_scale=HEAD_DIM ** -0.5),
        grid=(nb, seq // tm),
        in_specs=[pl.BlockSpec((1, tm, d), tok),
                  pl.BlockSpec((1,) + mod3.shape[1:], lambda b, i: (b, 0, 0)),
                  _resident((d, n_in)),
                  _resident((1, 2 * d)),
                  pl.BlockSpec((tm, LANES), tab),
                  pl.BlockSpec((tm, LANES), tab),
                  pl.BlockSpec((tm, LANES), tab)],
        out_specs=[pl.BlockSpec((1, tm, d), tok)] * 4 + [pl.BlockSpec((1, tm, 2 * d), tok)],
        out_shape=[act(d)] * 4 + [act(2 * d)],
        compiler_params=_compiler_params(2),
        name="in_projection",
    )(x, mod3, w_in, b_gate.reshape(1, 2 * d), *tables)


def _attn_kernel(q_ref, k_ref, v_ref, lam_ref, g_ref, o_ref, s_scr, p_scr,
                 *, key_chunk, out_scale):
    tq = q_ref.shape[1]
    seq = k_ref.shape[1]
    q = q_ref[0]
    lane = lax.broadcasted_iota(jnp.int32, q.shape, 1)
    zero = jnp.zeros_like(q)
    qq = jnp.concatenate([jnp.where(lane < HEAD_DIM, q, zero),
                          jnp.where(lane >= HEAD_DIM, q, zero)], axis=0)
    n_chunks = seq // key_chunk
    m_run = None
    for c in range(n_chunks):
        kc = k_ref[0, c * key_chunk:(c + 1) * key_chunk, :]
        s = lax.dot_general(qq, kc, (((1,), (1,)), ((), ())), preferred_element_type=F32)
        s_scr[:, c * key_chunk:(c + 1) * key_chunk] = s
        m_c = jnp.max(s, axis=-1, keepdims=True)
        m_run = m_c if m_run is None else jnp.maximum(m_run, m_c)
    l_run = None
    for c in range(n_chunks):
        p = jnp.exp(s_scr[:, c * key_chunk:(c + 1) * key_chunk] - m_run)
        l_c = jnp.sum(p, axis=-1, keepdims=True)
        l_run = l_c if l_run is None else l_run + l_c
        p_scr[:, c * key_chunk:(c + 1) * key_chunk] = p.astype(BF16)
    o = jnp.dot(p_scr[...], v_ref[0], preferred_element_type=F32) / l_run
    att = o[:tq] - lam_ref[...] * o[tq:]
    ms = jnp.mean(att * att, axis=-1, keepdims=True)
    y = att * lax.rsqrt(ms + LN_EPS) * g_ref[...] * out_scale
    o_ref[0] = y.astype(BF16)


def _attention(q, k, v, lam, subln_g, lam_init, tq):
    nb, seq, d = q.shape
    n_heads = d // HEAD_WIDTH
    key_chunk = min(512, seq)
    return pl.pallas_call(
        functools.partial(_attn_kernel, key_chunk=key_chunk, out_scale=1.0 - lam_init),
        grid=(nb, n_heads, seq // tq),
        in_specs=[pl.BlockSpec((1, tq, HEAD_WIDTH), lambda b, h, i: (b, i, h)),
                  pl.BlockSpec((1, seq, HEAD_WIDTH), lambda b, h, i: (b, 0, h)),
                  pl.BlockSpec((1, seq, HEAD_WIDTH), lambda b, h, i: (b, 0, h)),
                  pl.BlockSpec((1, LANES), lambda b, h, i: (0, 0)),
                  pl.BlockSpec((1, HEAD_WIDTH), lambda b, h, i: (0, 0))],
        out_specs=pl.BlockSpec((1, tq, HEAD_WIDTH), lambda b, h, i: (b, i, h)),
        out_shape=jax.ShapeDtypeStruct((nb, seq, d), BF16),
        scratch_shapes=[pltpu.VMEM((2 * tq, seq), F32),
                        pltpu.VMEM((2 * tq, seq), BF16)],
        compiler_params=_compiler_params(3),
        name="diff_attention",
    )(q, k, v, lam, subln_g.reshape(1, HEAD_WIDTH).astype(F32))


CONV_HALO = 16
CONV_ROWS = 64


def _mix_kernel(u_ref, up_ref, un_ref, att_ref, g_ref, x_ref, mod_ref,
                cw_ref, cb_ref, clg_ref, clb_ref, wpw_ref, bpw_ref, wo_ref,
                l1g_ref, l1b_ref, o_ref, ubuf, cbuf, *, alpha):
    tm, d = u_ref.shape[1], u_ref.shape[2]
    width = cw_ref.shape[0]
    pad = (width - 1) // 2
    i = pl.program_id(1)
    last = pl.num_programs(1) - 1
    ubuf[0:CONV_HALO, :] = jnp.where(i > 0, up_ref[0].astype(F32), 0.0)
    ubuf[CONV_HALO:CONV_HALO + tm, :] = u_ref[0].astype(F32)
    ubuf[CONV_HALO + tm:, :] = jnp.where(i < last, un_ref[0].astype(F32), 0.0)

    for r0 in range(0, tm, CONV_ROWS):
        for c in range(d // LANES):
            cols = slice(c * LANES, (c + 1) * LANES)
            acc = jnp.zeros((CONV_ROWS, LANES), F32)
            for j in range(width):
                start = r0 + CONV_HALO - pad + j
                acc = acc + ubuf[start:start + CONV_ROWS, cols] * cw_ref[j:j + 1, cols]
            cbuf[r0:r0 + CONV_ROWS, cols] = acc

    conv = cbuf[...] + cb_ref[...]
    y = _norm(conv) * clg_ref[...] + clb_ref[...]
    y = y * jax.nn.sigmoid(y)
    y_b = jnp.dot(y.astype(BF16), wpw_ref[...], preferred_element_type=F32) + bpw_ref[...]
    gates = g_ref[0]
    merged = gates[:, :d] * att_ref[0] + gates[:, d:] * y_b
    mix = jnp.dot(merged.astype(BF16), wo_ref[...], preferred_element_type=F32)
    g1 = mod_ref[0, 2:3, :]
    z = alpha * x_ref[0] + g1 * mix
    o_ref[0] = _norm(z) * l1g_ref[...] + l1b_ref[...]


def _mixer_output(u, att, gates, x, mod3, conv_w, conv_b, conv_ln_g, conv_ln_b,
                  w_pw2, b_pw2, w_o, ln1_g, ln1_b, alpha, tm):
    nb, seq, d = x.shape
    width = conv_w.shape[0]
    assert (width - 1) // 2 < CONV_HALO and tm % CONV_ROWS == 0 and tm % CONV_HALO == 0
    hb = tm // CONV_HALO
    n_halo = seq // CONV_HALO
    tok = lambda b, i: (b, i, 0)
    row = lambda v: v.reshape(1, d).astype(F32)
    vec = pl.BlockSpec((1, d), lambda b, i: (0, 0))
    return pl.pallas_call(
        functools.partial(_mix_kernel, alpha=alpha),
        grid=(nb, seq // tm),
        in_specs=[pl.BlockSpec((1, tm, d), tok),
                  pl.BlockSpec((1, CONV_HALO, d), lambda b, i: (b, jnp.maximum(i * hb - 1, 0), 0)),
                  pl.BlockSpec((1, CONV_HALO, d),
                               lambda b, i: (b, jnp.minimum((i + 1) * hb, n_halo - 1), 0)),
                  pl.BlockSpec((1, tm, d), tok),
                  pl.BlockSpec((1, tm, 2 * d), tok),
                  pl.BlockSpec((1, tm, d), tok),
                  pl.BlockSpec((1,) + mod3.shape[1:], lambda b, i: (b, 0, 0)),
                  _resident((width, d)), vec, vec, vec,
                  _resident((d, d)), vec,
                  _resident((d, d)), vec, vec],
        out_specs=pl.BlockSpec((1, tm, d), tok),
        out_shape=jax.ShapeDtypeStruct((nb, seq, d), F32),
        scratch_shapes=[pltpu.VMEM((tm + 2 * CONV_HALO, d), F32),
                        pltpu.VMEM((tm, d), F32)],
        compiler_params=_compiler_params(2),
        name="mixer_output",
    )(u, u, u, att, gates, x, mod3, conv_w.astype(F32), row(conv_b), row(conv_ln_g),
      row(conv_ln_b), w_pw2, row(b_pw2), w_o, row(ln1_g), row(ln1_b))


FFN_HALO = 8
FFN_CHUNK = 256


def _ffn_kernel(x_ref, xp_ref, xn_ref, mod_ref, wup_ref, fcw_ref, fcb_ref, wdn_ref,
                l2g_ref, l2b_ref, o_ref, gbuf, vbuf, abuf, *, alpha, d_ff):
    tm = x_ref.shape[1]
    i = pl.program_id(1)
    last = pl.num_programs(1) - 1
    sh2 = mod_ref[0, 3:4, :]
    sc2 = mod_ref[0, 4:5, :]
    g2 = mod_ref[0, 5:6, :]

    def modnorm(t):
        return _norm(t) * (1.0 + sc2) + sh2

    x = x_ref[0]
    h_prev = jnp.where(i > 0, modnorm(xp_ref[0]), 0.0)
    h_next = jnp.where(i < last, modnorm(xn_ref[0]), 0.0)
    h_ext = jnp.concatenate([h_prev, modnorm(x), h_next], axis=0).astype(BF16)

    def conv3(buf, c0):
        cols = slice(c0, c0 + FFN_CHUNK)
        return (buf[FFN_HALO - 1:FFN_HALO - 1 + tm, :] * fcw_ref[0:1, cols]
                + buf[FFN_HALO:FFN_HALO + tm, :] * fcw_ref[1:2, cols]
                + buf[FFN_HALO + 1:FFN_HALO + 1 + tm, :] * fcw_ref[2:3, cols]
                + fcb_ref[:, cols])

    for c in range(d_ff // FFN_CHUNK):
        c0 = c * FFN_CHUNK
        gbuf[...] = jnp.dot(h_ext, wup_ref[:, c0:c0 + FFN_CHUNK], preferred_element_type=F32)
        vbuf[...] = jnp.dot(h_ext, wup_ref[:, d_ff + c0:d_ff + c0 + FFN_CHUNK],
                            preferred_element_type=F32)
        fg = conv3(gbuf, c0)
        fv = conv3(vbuf, d_ff + c0)
        gelu = 0.5 * fg * (1.0 + lax.erf(fg * (2.0 ** -0.5)))
        abuf[:, c0:c0 + FFN_CHUNK] = (gelu * fv).astype(BF16)

    f = jnp.dot(abuf[...], wdn_ref[...], preferred_element_type=F32)
    z = alpha * x + g2 * f
    o_ref[0] = _norm(z) * l2g_ref[...] + l2b_ref[...]


def _channel_mixer(x1, mod3, w_up, ffn_conv_w, ffn_conv_b, w_down, ln2_g, ln2_b, alpha, tm):
    nb, seq, d = x1.shape
    d_ff = w_down.shape[0]
    assert d_ff % FFN_CHUNK == 0 and ffn_conv_w.shape[0] == 3 and tm % FFN_HALO == 0
    hb = tm // FFN_HALO
    n_halo = seq // FFN_HALO
    tok = lambda b, i: (b, i, 0)
    vec = pl.BlockSpec((1, d), lambda b, i: (0, 0))
    return pl.pallas_call(
        functools.partial(_ffn_kernel, alpha=alpha, d_ff=d_ff),
        grid=(nb, seq // tm),
        in_specs=[pl.BlockSpec((1, tm, d), tok),
                  pl.BlockSpec((1, FFN_HALO, d), lambda b, i: (b, jnp.maximum(i * hb - 1, 0), 0)),
                  pl.BlockSpec((1, FFN_HALO, d),
                               lambda b, i: (b, jnp.minimum((i + 1) * hb, n_halo - 1), 0)),
                  pl.BlockSpec((1,) + mod3.shape[1:], lambda b, i: (b, 0, 0)),
                  _resident((d, 2 * d_ff)),
                  _resident((3, 2 * d_ff)),
                  _resident((1, 2 * d_ff)),
                  _resident((d_ff, d)), vec, vec],
        out_specs=pl.BlockSpec((1, tm, d), tok),
        out_shape=jax.ShapeDtypeStruct((nb, seq, d), F32),
        scratch_shapes=[pltpu.VMEM((tm + 2 * FFN_HALO, FFN_CHUNK), F32),
                        pltpu.VMEM((tm + 2 * FFN_HALO, FFN_CHUNK), F32),
                        pltpu.VMEM((tm, d_ff), BF16)],
        compiler_params=_compiler_params(2),
        name="channel_mixer",
    )(x1, x1, x1, mod3, w_up, ffn_conv_w.astype(F32), ffn_conv_b.reshape(1, 2 * d_ff).astype(F32),
      w_down, ln2_g.reshape(1, d).astype(F32), ln2_b.reshape(1, d).astype(F32))


def _encoder_layer(x, mod3, lam, lam_init, alpha, w_in, b_gate, subln_g, conv_w, conv_b,
                   conv_ln_g, conv_ln_b, w_pw2, b_pw2, w_o, ln1_g, ln1_b, w_up, ffn_conv_w,
                   ffn_conv_b, w_down, ln2_g, ln2_b):
    seq = x.shape[1]
    tm, tq = _tiles(seq)
    q, k, v, u, gates = _in_projection(x, mod3, w_in, b_gate, _rope_tables(seq), tm)
    att = _attention(q, k, v, lam, subln_g, lam_init, tq)
    x1 = _mixer_output(u, att, gates, x, mod3, conv_w, conv_b, conv_ln_g, conv_ln_b,
                       w_pw2, b_pw2, w_o, ln1_g, ln1_b, alpha, tm)
    return _channel_mixer(x1, mod3, w_up, ffn_conv_w, ffn_conv_b, w_down, ln2_g, ln2_b, alpha, tm)


def kernel(x_prompt, x_sample, c_prompt, c_sample, w_ada, b_ada, w_in, b_gate, lambda_q1, lambda_k1, lambda_q2, lambda_k2, subln_g, conv_w, conv_b, conv_ln_g, conv_ln_b, w_pw2, b_pw2, w_o, ln1_g, ln1_b, w_up, ffn_conv_w, ffn_conv_b, w_down, ln2_g, ln2_b):
    depth = w_ada.shape[0]
    d = x_prompt.shape[-1]
    alpha = (2.0 * depth) ** 0.25
    n_prompt = c_prompt.shape[0]
    y_prompt, y_sample = x_prompt, x_sample
    for l in range(depth):
        lam_init = 0.8 - 0.6 * math.exp(-0.3 * l)
        mod = _modulation(jnp.concatenate([c_prompt, c_sample], axis=0), w_ada[l], b_ada[l])
        mod3 = mod.reshape(mod.shape[0], -1, d)
        lam = _lambda(lambda_q1[l], lambda_k1[l], lambda_q2[l], lambda_k2[l], lam_init)
        weights = (w_in[l].astype(BF16), b_gate[l], subln_g[l], conv_w[l], conv_b[l],
                   conv_ln_g[l], conv_ln_b[l], w_pw2[l].astype(BF16), b_pw2[l],
                   w_o[l].astype(BF16), ln1_g[l], ln1_b[l], w_up[l].astype(BF16),
                   ffn_conv_w[l], ffn_conv_b[l], w_down[l].astype(BF16), ln2_g[l], ln2_b[l])
        y_prompt = _encoder_layer(y_prompt, mod3[:n_prompt], lam, lam_init, alpha, *weights)
        y_sample = _encoder_layer(y_sample, mod3[n_prompt:], lam, lam_init, alpha, *weights)
    return (y_prompt, y_sample)
```

```python
import functools
import math

import jax
import jax.numpy as jnp
from jax import lax
from jax.experimental import pallas as pl
from jax.experimental.pallas import tpu as pltpu

F32 = jnp.float32
BF16 = jnp.bfloat16

HEAD_DIM = 64
HEAD_WIDTH = 2 * HEAD_DIM
ROT_DIM = HEAD_DIM // 4
ROPE_THETA = 500000.0
LN_EPS = 1e-5
LANES = 128
VMEM_LIMIT_BYTES = 56 * 1024 * 1024


def _compiler_params(n_grid_dims):
    return pltpu.CompilerParams(
        dimension_semantics=("arbitrary",) * n_grid_dims,
        vmem_limit_bytes=VMEM_LIMIT_BYTES)


def _resident(shape):
    zeros = (0,) * len(shape)
    return pl.BlockSpec(shape, lambda *_: zeros, pipeline_mode=pl.Buffered(1))


def _tiles(seq):
    tm = min(512, seq)
    tq = min(256, seq // 2)
    return tm, tq


def _norm(x):
    mu = jnp.mean(x, axis=-1, keepdims=True)
    xc = x - mu
    var = jnp.mean(xc * xc, axis=-1, keepdims=True)
    return xc * lax.rsqrt(var + LN_EPS)


def _mod_kernel(c_ref, w_ref, b_ref, o_ref):
    c = c_ref[...]
    a = c * jax.nn.sigmoid(c)
    o_ref[...] = jnp.dot(a, w_ref[...], preferred_element_type=F32,
                         precision=lax.Precision.HIGHEST) + b_ref[...]


def _modulation(c, w_ada, b_ada):
    nb, d = c.shape
    n_out = w_ada.shape[1]
    return pl.pallas_call(
        _mod_kernel,
        grid=(n_out // d,),
        in_specs=[pl.BlockSpec((nb, d), lambda j: (0, 0)),
                  pl.BlockSpec((d, d), lambda j: (0, j)),
                  pl.BlockSpec((1, d), lambda j: (0, j))],
        out_specs=pl.BlockSpec((nb, d), lambda j: (0, j)),
        out_shape=jax.ShapeDtypeStruct((nb, n_out), F32),
        compiler_params=_compiler_params(1),
        name="adaln_modulation",
    )(c, w_ada, b_ada.reshape(1, n_out))


def _lambda_kernel(l_ref, o_ref, *, lam_init):
    a = jnp.sum(l_ref[0:1, :] * l_ref[1:2, :], axis=-1, keepdims=True)
    b = jnp.sum(l_ref[2:3, :] * l_ref[3:4, :], axis=-1, keepdims=True)
    lam = jnp.exp(a) - jnp.exp(b) + lam_init
    o_ref[...] = jnp.broadcast_to(lam, o_ref.shape)


def _lambda(lq1, lk1, lq2, lk2, lam_init):
    stacked = jnp.stack([lq1, lk1, lq2, lk2]).astype(F32)
    return pl.pallas_call(
        functools.partial(_lambda_kernel, lam_init=lam_init),
        out_shape=jax.ShapeDtypeStruct((1, LANES), F32),
        name="diff_lambda",
    )(stacked)


def _rope_tables(seq):
    half = ROT_DIM // 2
    inv_freq = ROPE_THETA ** (-jnp.arange(0, ROT_DIM, 2, dtype=F32) / ROT_DIM)
    ang = jnp.arange(seq, dtype=F32)[:, None] * inv_freq[None, :]
    cos, sin = jnp.cos(ang), jnp.sin(ang)
    ones = jnp.ones((seq, HEAD_DIM - ROT_DIM), F32)
    zeros = jnp.zeros((seq, HEAD_DIM - ROT_DIM), F32)
    zh = jnp.zeros((seq, half), F32)
    cos_t = jnp.concatenate([cos, cos, ones], axis=1)
    sin_a = jnp.concatenate([-sin, zh, zeros], axis=1)
    sin_b = jnp.concatenate([zh, sin, zeros], axis=1)
    rep = LANES // HEAD_DIM
    return tuple(jnp.tile(t, (1, rep)) for t in (cos_t, sin_a, sin_b))


def _inproj_kernel(x_ref, mod_ref, w_ref, wvt_ref, bg_ref, cos_ref, sa_ref, sb_ref,
                   q_ref, k_ref, vt_ref, u_ref, g_ref, *, d, qk_scale):
    half = ROT_DIM // 2
    sh1 = mod_ref[0, 0:1, :]
    sc1 = mod_ref[0, 1:2, :]
    h = (_norm(x_ref[0]) * (1.0 + sc1) + sh1).astype(BF16)

    def proj(c0, width):
        return jnp.dot(h, w_ref[:, c0:c0 + width], preferred_element_type=F32)

    cos_t, sin_a, sin_b = cos_ref[...], sa_ref[...], sb_ref[...]

    def store_rope(t, o_ref, scale):
        for j in range(d // LANES):
            s = t[:, j * LANES:(j + 1) * LANES]
            r = (s * cos_t + pltpu.roll(s, LANES - half, 1) * sin_a
                 + pltpu.roll(s, half, 1) * sin_b)
            if scale != 1.0:
                r = r * scale
            o_ref[0, :, j * LANES:(j + 1) * LANES] = r.astype(BF16)

    store_rope(proj(0, d), q_ref, qk_scale)
    store_rope(proj(d, d), k_ref, 1.0)
    vt_ref[0] = lax.dot_general(wvt_ref[...], h, (((1,), (1,)), ((), ())),
                                preferred_element_type=F32).astype(BF16)
    glu_a = proj(3 * d, d)
    glu_b = proj(4 * d, d)
    u_ref[0] = (glu_a * jax.nn.sigmoid(glu_b)).astype(BF16)
    gate = proj(5 * d, 2 * d) + bg_ref[...]
    g_ref[0] = jax.nn.sigmoid(gate).astype(BF16)


def _in_projection(x, mod3, w_in, w_v_t, b_gate, tables, tm):
    nb, seq, d = x.shape
    n_in = w_in.shape[1]
    tok = lambda b, i: (b, i, 0)
    tab = lambda b, i: (i, 0)
    act = lambda width: jax.ShapeDtypeStruct((nb, seq, width), BF16)
    tok_spec = pl.BlockSpec((1, tm, d), tok)
    return pl.pallas_call(
        functools.partial(_inproj_kernel, d=d, qk_scale=HEAD_DIM ** -0.5 * math.log2(math.e)),
        grid=(nb, seq // tm),
        in_specs=[tok_spec,
                  pl.BlockSpec((1,) + mod3.shape[1:], lambda b, i: (b, 0, 0)),
                  _resident((d, n_in)),
                  _resident((d, d)),
                  _resident((1, 2 * d)),
                  pl.BlockSpec((tm, LANES), tab),
                  pl.BlockSpec((tm, LANES), tab),
                  pl.BlockSpec((tm, LANES), tab)],
        out_specs=[tok_spec, tok_spec, pl.BlockSpec((1, d, tm), lambda b, i: (b, 0, i)),
                   tok_spec, pl.BlockSpec((1, tm, 2 * d), tok)],
        out_shape=[act(d), act(d), jax.ShapeDtypeStruct((nb, d, seq), BF16), act(d), act(2 * d)],
        compiler_params=_compiler_params(2),
        name="in_projection",
    )(x, mod3, w_in, w_v_t, b_gate.reshape(1, 2 * d), *tables)


ATTN_KEY_CHUNK = 512
ATTN_KEY_ROWS = 64
SUBLANES = 8


def _attn_kernel(q_ref, k_ref, vt_ref, lam_ref, g_ref, o_ref,
                 s_a, s_b, p_a, p_b, m_a, m_b, l_a, l_b, *, tq, out_scale):
    seq = k_ref.shape[1]
    n_blk = seq // tq
    key_chunk = min(ATTN_KEY_CHUNK, seq)
    cols = 2 * tq

    def block_start(j):
        return j * tq if isinstance(j, int) else pl.multiple_of(j * tq, tq)

    def fold_rows(t, op):
        acc = t[0:SUBLANES]
        for r0 in range(SUBLANES, t.shape[0], SUBLANES):
            acc = op(acc, t[r0:r0 + SUBLANES])
        return acc

    def all_rows(t, reduce):
        return jnp.broadcast_to(reduce(t, axis=0, keepdims=True), t.shape)

    def scores(j, s_buf, m_buf):
        q = q_ref[0, pl.ds(block_start(j), tq), :]
        lane = lax.broadcasted_iota(jnp.int32, q.shape, 1)
        zero = jnp.zeros_like(q)
        qq = jnp.concatenate([jnp.where(lane < HEAD_DIM, q, zero),
                              jnp.where(lane >= HEAD_DIM, q, zero)], axis=0)
        m_acc = None
        for c0 in range(0, seq, key_chunk):
            st = lax.dot_general(k_ref[0, c0:c0 + key_chunk, :], qq, (((1,), (1,)), ((), ())),
                                 preferred_element_type=F32)
            s_buf[c0:c0 + key_chunk, :] = st
            m_c = fold_rows(st, jnp.maximum)
            m_acc = m_c if m_acc is None else jnp.maximum(m_acc, m_c)
        m_buf[...] = all_rows(m_acc, jnp.max)

    def softmax(s_buf, m_buf, p_buf, l_buf):
        m = m_buf[...]
        l_acc = None
        for r0 in range(0, seq, ATTN_KEY_ROWS):
            parts = []
            for r in range(r0, r0 + ATTN_KEY_ROWS, SUBLANES):
                p = jnp.exp2(s_buf[r:r + SUBLANES, :] - m)
                l_acc = p if l_acc is None else l_acc + p
                parts.append(p)
            p_buf[r0:r0 + ATTN_KEY_ROWS, :] = jnp.concatenate(parts, axis=0).astype(BF16)
        l_buf[...] = all_rows(l_acc, jnp.sum)

    def values(j, p_buf, l_buf):
        ot = jnp.dot(vt_ref[0], p_buf[...], preferred_element_type=F32)
        ot = ot / l_buf[0:1, :]
        lam = jnp.concatenate([lam_ref[...]] * (tq // LANES), axis=1)
        att = ot[:, :tq] - lam * ot[:, tq:]
        ms = jnp.mean(att * att, axis=0, keepdims=True)
        y = att * lax.rsqrt(ms + LN_EPS) * g_ref[...] * out_scale
        o_ref[0, pl.ds(block_start(j), tq), :] = y.T.astype(BF16)

    scores(0, s_a, m_a)
    scores(1, s_b, m_b)
    softmax(s_a, m_a, p_a, l_a)

    def pair(jj, carry):
        j = 2 * jj
        scores(j + 2, s_a, m_a)
        softmax(s_b, m_b, p_b, l_b)
        values(j, p_a, l_a)
        scores(j + 3, s_b, m_b)
        softmax(s_a, m_a, p_a, l_a)
        values(j + 1, p_b, l_b)
        return carry

    lax.fori_loop(0, n_blk // 2 - 1, pair, 0)
    softmax(s_b, m_b, p_b, l_b)
    values(n_blk - 2, p_a, l_a)
    values(n_blk - 1, p_b, l_b)


def _attention(q, k, v_t, lam, subln_g, lam_init, tq):
    nb, seq, d = q.shape
    n_heads = d // HEAD_WIDTH
    assert (seq // tq) % 2 == 0 and tq % LANES == 0 and seq % ATTN_KEY_ROWS == 0
    head = pl.BlockSpec((1, seq, HEAD_WIDTH), lambda b, h: (b, 0, h))
    gain = jnp.broadcast_to(subln_g.astype(F32)[:, None], (HEAD_WIDTH, tq))
    score_buf = pltpu.VMEM((seq, 2 * tq), F32)
    prob_buf = pltpu.VMEM((seq, 2 * tq), BF16)
    stat_buf = pltpu.VMEM((SUBLANES, 2 * tq), F32)
    return pl.pallas_call(
        functools.partial(_attn_kernel, tq=tq, out_scale=1.0 - lam_init),
        grid=(nb, n_heads),
        in_specs=[head, head,
                  pl.BlockSpec((1, HEAD_WIDTH, seq), lambda b, h: (b, h, 0)),
                  pl.BlockSpec((1, LANES), lambda b, h: (0, 0)),
                  pl.BlockSpec((HEAD_WIDTH, tq), lambda b, h: (0, 0))],
        out_specs=head,
        out_shape=jax.ShapeDtypeStruct((nb, seq, d), BF16),
        scratch_shapes=[score_buf, score_buf, prob_buf, prob_buf,
                        stat_buf, stat_buf, stat_buf, stat_buf],
        compiler_params=_compiler_params(2),
        name="diff_attention",
    )(q, k, v_t, lam, gain)


CONV_HALO = 16
CONV_ROWS = 64


def _mix_kernel(u_ref, up_ref, un_ref, att_ref, g_ref, x_ref, mod_ref,
                cw_ref, cb_ref, clg_ref, clb_ref, wpw_ref, bpw_ref, wo_ref,
                l1g_ref, l1b_ref, o_ref, ubuf, cbuf, *, alpha):
    tm, d = u_ref.shape[1], u_ref.shape[2]
    width = cw_ref.shape[0]
    pad = (width - 1) // 2
    i = pl.program_id(1)
    last = pl.num_programs(1) - 1
    ubuf[0:CONV_HALO, :] = jnp.where(i > 0, up_ref[0].astype(F32), 0.0)
    ubuf[CONV_HALO:CONV_HALO + tm, :] = u_ref[0].astype(F32)
    ubuf[CONV_HALO + tm:, :] = jnp.where(i < last, un_ref[0].astype(F32), 0.0)

    for r0 in range(0, tm, CONV_ROWS):
        for c in range(d // LANES):
            cols = slice(c * LANES, (c + 1) * LANES)
            acc = jnp.zeros((CONV_ROWS, LANES), F32)
            for j in range(width):
                start = r0 + CONV_HALO - pad + j
                acc = acc + ubuf[start:start + CONV_ROWS, cols] * cw_ref[j:j + 1, cols]
            cbuf[r0:r0 + CONV_ROWS, cols] = acc

    conv = cbuf[...] + cb_ref[...]
    y = _norm(conv) * clg_ref[...] + clb_ref[...]
    y = y * jax.nn.sigmoid(y)
    y_b = jnp.dot(y.astype(BF16), wpw_ref[...], preferred_element_type=F32) + bpw_ref[...]
    gates = g_ref[0]
    merged = gates[:, :d] * att_ref[0] + gates[:, d:] * y_b
    mix = jnp.dot(merged.astype(BF16), wo_ref[...], preferred_element_type=F32)
    g1 = mod_ref[0, 2:3, :]
    z = alpha * x_ref[0] + g1 * mix
    o_ref[0] = _norm(z) * l1g_ref[...] + l1b_ref[...]


def _mixer_output(u, att, gates, x, mod3, conv_w, conv_b, conv_ln_g, conv_ln_b,
                  w_pw2, b_pw2, w_o, ln1_g, ln1_b, alpha, tm):
    nb, seq, d = x.shape
    width = conv_w.shape[0]
    assert (width - 1) // 2 < CONV_HALO and tm % CONV_ROWS == 0 and tm % CONV_HALO == 0
    hb = tm // CONV_HALO
    n_halo = seq // CONV_HALO
    tok = lambda b, i: (b, i, 0)
    row = lambda v: v.reshape(1, d).astype(F32)
    vec = pl.BlockSpec((1, d), lambda b, i: (0, 0))
    return pl.pallas_call(
        functools.partial(_mix_kernel, alpha=alpha),
        grid=(nb, seq // tm),
        in_specs=[pl.BlockSpec((1, tm, d), tok),
                  pl.BlockSpec((1, CONV_HALO, d), lambda b, i: (b, jnp.maximum(i * hb - 1, 0), 0)),
                  pl.BlockSpec((1, CONV_HALO, d),
                               lambda b, i: (b, jnp.minimum((i + 1) * hb, n_halo - 1), 0)),
                  pl.BlockSpec((1, tm, d), tok),
                  pl.BlockSpec((1, tm, 2 * d), tok),
                  pl.BlockSpec((1, tm, d), tok),
                  pl.BlockSpec((1,) + mod3.shape[1:], lambda b, i: (b, 0, 0)),
                  _resident((width, d)), vec, vec, vec,
                  _resident((d, d)), vec,
                  _resident((d, d)), vec, vec],
        out_specs=pl.BlockSpec((1, tm, d), tok),
        out_shape=jax.ShapeDtypeStruct((nb, seq, d), F32),
        scratch_shapes=[pltpu.VMEM((tm + 2 * CONV_HALO, d), F32),
                        pltpu.VMEM((tm, d), F32)],
        compiler_params=_compiler_params(2),
        name="mixer_output",
    )(u, u, u, att, gates, x, mod3, conv_w.astype(F32), row(conv_b), row(conv_ln_g),
      row(conv_ln_b), w_pw2, row(b_pw2), w_o, row(ln1_g), row(ln1_b))


FFN_HALO = 8
FFN_CHUNK = 256


def _ffn_kernel(x_ref, xp_ref, xn_ref, mod_ref, wup_ref, fcw_ref, fcb_ref, wdn_ref,
                l2g_ref, l2b_ref, o_ref, gbuf, vbuf, abuf, *, alpha, d_ff):
    tm = x_ref.shape[1]
    i = pl.program_id(1)
    last = pl.num_programs(1) - 1
    sh2 = mod_ref[0, 3:4, :]
    sc2 = mod_ref[0, 4:5, :]
    g2 = mod_ref[0, 5:6, :]

    def modnorm(t):
        return _norm(t) * (1.0 + sc2) + sh2

    x = x_ref[0]
    h_prev = jnp.where(i > 0, modnorm(xp_ref[0]), 0.0)
    h_next = jnp.where(i < last, modnorm(xn_ref[0]), 0.0)
    h_ext = jnp.concatenate([h_prev, modnorm(x), h_next], axis=0).astype(BF16)

    def conv3(buf, c0):
        cols = slice(c0, c0 + FFN_CHUNK)
        return (buf[FFN_HALO - 1:FFN_HALO - 1 + tm, :] * fcw_ref[0:1, cols]
                + buf[FFN_HALO:FFN_HALO + tm, :] * fcw_ref[1:2, cols]
                + buf[FFN_HALO + 1:FFN_HALO + 1 + tm, :] * fcw_ref[2:3, cols]
                + fcb_ref[:, cols])

    for c in range(d_ff // FFN_CHUNK):
        c0 = c * FFN_CHUNK
        gbuf[...] = jnp.dot(h_ext, wup_ref[:, c0:c0 + FFN_CHUNK], preferred_element_type=F32)
        vbuf[...] = jnp.dot(h_ext, wup_ref[:, d_ff + c0:d_ff + c0 + FFN_CHUNK],
                            preferred_element_type=F32)
        fg = conv3(gbuf, c0)
        fv = conv3(vbuf, d_ff + c0)
        gelu = 0.5 * fg * (1.0 + lax.erf(fg * (2.0 ** -0.5)))
        abuf[:, c0:c0 + FFN_CHUNK] = (gelu * fv).astype(BF16)

    f = jnp.dot(abuf[...], wdn_ref[...], preferred_element_type=F32)
    z = alpha * x + g2 * f
    o_ref[0] = _norm(z) * l2g_ref[...] + l2b_ref[...]


def _channel_mixer(x1, mod3, w_up, ffn_conv_w, ffn_conv_b, w_down, ln2_g, ln2_b, alpha, tm):
    nb, seq, d = x1.shape
    d_ff = w_down.shape[0]
    assert d_ff % FFN_CHUNK == 0 and ffn_conv_w.shape[0] == 3 and tm % FFN_HALO == 0
    hb = tm // FFN_HALO
    n_halo = seq // FFN_HALO
    tok = lambda b, i: (b, i, 0)
    vec = pl.BlockSpec((1, d), lambda b, i: (0, 0))
    return pl.pallas_call(
        functools.partial(_ffn_kernel, alpha=alpha, d_ff=d_ff),
        grid=(nb, seq // tm),
        in_specs=[pl.BlockSpec((1, tm, d), tok),
                  pl.BlockSpec((1, FFN_HALO, d), lambda b, i: (b, jnp.maximum(i * hb - 1, 0), 0)),
                  pl.BlockSpec((1, FFN_HALO, d),
                               lambda b, i: (b, jnp.minimum((i + 1) * hb, n_halo - 1), 0)),
                  pl.BlockSpec((1,) + mod3.shape[1:], lambda b, i: (b, 0, 0)),
                  _resident((d, 2 * d_ff)),
                  _resident((3, 2 * d_ff)),
                  _resident((1, 2 * d_ff)),
                  _resident((d_ff, d)), vec, vec],
        out_specs=pl.BlockSpec((1, tm, d), tok),
        out_shape=jax.ShapeDtypeStruct((nb, seq, d), F32),
        scratch_shapes=[pltpu.VMEM((tm + 2 * FFN_HALO, FFN_CHUNK), F32),
                        pltpu.VMEM((tm + 2 * FFN_HALO, FFN_CHUNK), F32),
                        pltpu.VMEM((tm, d_ff), BF16)],
        compiler_params=_compiler_params(2),
        name="channel_mixer",
    )(x1, x1, x1, mod3, w_up, ffn_conv_w.astype(F32), ffn_conv_b.reshape(1, 2 * d_ff).astype(F32),
      w_down, ln2_g.reshape(1, d).astype(F32), ln2_b.reshape(1, d).astype(F32))


def _encoder_layer(x, mod3, lam, lam_init, alpha, w_in, w_v_t, b_gate, subln_g, conv_w, conv_b,
                   conv_ln_g, conv_ln_b, w_pw2, b_pw2, w_o, ln1_g, ln1_b, w_up, ffn_conv_w,
                   ffn_conv_b, w_down, ln2_g, ln2_b):
    seq = x.shape[1]
    tm, tq = _tiles(seq)
    q, k, v_t, u, gates = _in_projection(x, mod3, w_in, w_v_t, b_gate, _rope_tables(seq), tm)
    att = _attention(q, k, v_t, lam, subln_g, lam_init, tq)
    x1 = _mixer_output(u, att, gates, x, mod3, conv_w, conv_b, conv_ln_g, conv_ln_b,
                       w_pw2, b_pw2, w_o, ln1_g, ln1_b, alpha, tm)
    return _channel_mixer(x1, mod3, w_up, ffn_conv_w, ffn_conv_b, w_down, ln2_g, ln2_b, alpha, tm)


def kernel(x_prompt, x_sample, c_prompt, c_sample, w_ada, b_ada, w_in, b_gate, lambda_q1, lambda_k1, lambda_q2, lambda_k2, subln_g, conv_w, conv_b, conv_ln_g, conv_ln_b, w_pw2, b_pw2, w_o, ln1_g, ln1_b, w_up, ffn_conv_w, ffn_conv_b, w_down, ln2_g, ln2_b):
    depth = w_ada.shape[0]
    d = x_prompt.shape[-1]
    alpha = (2.0 * depth) ** 0.25
    n_prompt = c_prompt.shape[0]
    y_prompt, y_sample = x_prompt, x_sample
    for l in range(depth):
        lam_init = 0.8 - 0.6 * math.exp(-0.3 * l)
        mod = _modulation(jnp.concatenate([c_prompt, c_sample], axis=0), w_ada[l], b_ada[l])
        mod3 = mod.reshape(mod.shape[0], -1, d)
        lam = _lambda(lambda_q1[l], lambda_k1[l], lambda_q2[l], lambda_k2[l], lam_init)
        w_v_t = w_in[l][:, 2 * d:3 * d].T.astype(BF16)
        weights = (w_in[l].astype(BF16), w_v_t, b_gate[l], subln_g[l], conv_w[l], conv_b[l],
                   conv_ln_g[l], conv_ln_b[l], w_pw2[l].astype(BF16), b_pw2[l],
                   w_o[l].astype(BF16), ln1_g[l], ln1_b[l], w_up[l].astype(BF16),
                   ffn_conv_w[l], ffn_conv_b[l], w_down[l].astype(BF16), ln2_g[l], ln2_b[l])
        y_prompt = _encoder_layer(y_prompt, mod3[:n_prompt], lam, lam_init, alpha, *weights)
        y_sample = _encoder_layer(y_sample, mod3[n_prompt:], lam, lam_init, alpha, *weights)
    return (y_prompt, y_sample)
```

```python
import functools
import math

import jax
import jax.numpy as jnp
from jax import lax
from jax.experimental import pallas as pl
from jax.experimental.pallas import tpu as pltpu

F32 = jnp.float32
BF16 = jnp.bfloat16

HEAD_DIM = 64
HEAD_WIDTH = 2 * HEAD_DIM
ROT_DIM = HEAD_DIM // 4
ROPE_THETA = 500000.0
LN_EPS = 1e-5
LANES = 128
SUBLANES = 8
SUM_ROWS = 16
V_ROWS = HEAD_WIDTH + SUM_ROWS
VMEM_LIMIT_BYTES = 56 * 1024 * 1024
ATTN_VMEM_BUDGET = 48 * 1024 * 1024


def _compiler_params(n_grid_dims):
    return pltpu.CompilerParams(
        dimension_semantics=("arbitrary",) * n_grid_dims,
        vmem_limit_bytes=VMEM_LIMIT_BYTES)


def _resident(shape):
    zeros = (0,) * len(shape)
    return pl.BlockSpec(shape, lambda *_: zeros, pipeline_mode=pl.Buffered(1))


def _tiles(seq):
    tm = min(512, seq)
    tq = min(256, seq // 2)
    return tm, tq


def _heads_per_step(seq, tq, n_heads):
    scratch = 2 * seq * 2 * tq * (4 + 2)
    per_head = 2 * seq * (3 * HEAD_WIDTH + V_ROWS) * 2
    heads = n_heads
    while heads > 1 and scratch + heads * per_head > ATTN_VMEM_BUDGET:
        heads //= 2
    return heads


def _norm(x):
    mu = jnp.mean(x, axis=-1, keepdims=True)
    xc = x - mu
    var = jnp.mean(xc * xc, axis=-1, keepdims=True)
    return xc * lax.rsqrt(var + LN_EPS)


def _mod_kernel(c_ref, w_ref, b_ref, o_ref):
    c = c_ref[...]
    a = c * jax.nn.sigmoid(c)
    o_ref[...] = jnp.dot(a, w_ref[...], preferred_element_type=F32,
                         precision=lax.Precision.HIGHEST) + b_ref[...]


def _modulation(c, w_ada, b_ada):
    nb, d = c.shape
    n_out = w_ada.shape[1]
    return pl.pallas_call(
        _mod_kernel,
        grid=(n_out // d,),
        in_specs=[pl.BlockSpec((nb, d), lambda j: (0, 0)),
                  pl.BlockSpec((d, d), lambda j: (0, j)),
                  pl.BlockSpec((1, d), lambda j: (0, j))],
        out_specs=pl.BlockSpec((nb, d), lambda j: (0, j)),
        out_shape=jax.ShapeDtypeStruct((nb, n_out), F32),
        compiler_params=_compiler_params(1),
        name="adaln_modulation",
    )(c, w_ada, b_ada.reshape(1, n_out))


def _lambda_kernel(l_ref, o_ref, *, lam_init):
    a = jnp.sum(l_ref[0:1, :] * l_ref[1:2, :], axis=-1, keepdims=True)
    b = jnp.sum(l_ref[2:3, :] * l_ref[3:4, :], axis=-1, keepdims=True)
    lam = jnp.exp(a) - jnp.exp(b) + lam_init
    o_ref[...] = jnp.broadcast_to(lam, o_ref.shape)


def _lambda(lq1, lk1, lq2, lk2, lam_init):
    stacked = jnp.stack([lq1, lk1, lq2, lk2]).astype(F32)
    return pl.pallas_call(
        functools.partial(_lambda_kernel, lam_init=lam_init),
        out_shape=jax.ShapeDtypeStruct((1, LANES), F32),
        name="diff_lambda",
    )(stacked)


def _rope_tables(seq):
    half = ROT_DIM // 2
    inv_freq = ROPE_THETA ** (-jnp.arange(0, ROT_DIM, 2, dtype=F32) / ROT_DIM)
    ang = jnp.arange(seq, dtype=F32)[:, None] * inv_freq[None, :]
    cos, sin = jnp.cos(ang), jnp.sin(ang)
    ones = jnp.ones((seq, HEAD_DIM - ROT_DIM), F32)
    zeros = jnp.zeros((seq, HEAD_DIM - ROT_DIM), F32)
    zh = jnp.zeros((seq, half), F32)
    cos_t = jnp.concatenate([cos, cos, ones], axis=1)
    sin_a = jnp.concatenate([-sin, zh, zeros], axis=1)
    sin_b = jnp.concatenate([zh, sin, zeros], axis=1)
    rep = LANES // HEAD_DIM
    return tuple(jnp.tile(t, (1, rep)) for t in (cos_t, sin_a, sin_b))


def _inproj_kernel(x_ref, mod_ref, w_ref, wvt_ref, bg_ref, cos_ref, sa_ref, sb_ref,
                   q_ref, k_ref, vt_ref, u_ref, g_ref, *, d, qk_scale):
    half = ROT_DIM // 2
    sh1 = mod_ref[0, 0:1, :]
    sc1 = mod_ref[0, 1:2, :]
    h = (_norm(x_ref[0]) * (1.0 + sc1) + sh1).astype(BF16)

    def proj(c0, width):
        return jnp.dot(h, w_ref[:, c0:c0 + width], preferred_element_type=F32)

    cos_t, sin_a, sin_b = cos_ref[...], sa_ref[...], sb_ref[...]

    def store_rope(t, o_ref, scale):
        for j in range(d // HEAD_WIDTH):
            s = t[:, j * HEAD_WIDTH:(j + 1) * HEAD_WIDTH]
            r = (s * cos_t + pltpu.roll(s, LANES - half, 1) * sin_a
                 + pltpu.roll(s, half, 1) * sin_b)
            if scale != 1.0:
                r = r * scale
            o_ref[0, j] = r.astype(BF16)

    store_rope(proj(0, d), q_ref, qk_scale)
    store_rope(proj(d, d), k_ref, 1.0)
    v_t = lax.dot_general(wvt_ref[...], h, (((1,), (1,)), ((), ())),
                          preferred_element_type=F32).astype(BF16)
    n_heads = d // HEAD_WIDTH
    vt_ref[0, :, 0:HEAD_WIDTH, :] = v_t.reshape(n_heads, HEAD_WIDTH, v_t.shape[1])
    vt_ref[0, :, HEAD_WIDTH:, :] = jnp.ones((n_heads, SUM_ROWS, v_t.shape[1]), BF16)
    glu_a = proj(3 * d, d)
    glu_b = proj(4 * d, d)
    u_ref[0] = (glu_a * jax.nn.sigmoid(glu_b)).astype(BF16)
    gate = proj(5 * d, 2 * d) + bg_ref[...]
    g_ref[0] = jax.nn.sigmoid(gate).astype(BF16)


def _in_projection(x, mod3, w_in, w_v_t, b_gate, tables, tm):
    nb, seq, d = x.shape
    n_in = w_in.shape[1]
    n_heads = d // HEAD_WIDTH
    assert HEAD_WIDTH == LANES
    tok = lambda b, i: (b, i, 0)
    tab = lambda b, i: (i, 0)
    act = lambda width: jax.ShapeDtypeStruct((nb, seq, width), BF16)
    tok_spec = pl.BlockSpec((1, tm, d), tok)
    head_spec = pl.BlockSpec((1, n_heads, tm, HEAD_WIDTH), lambda b, i: (b, 0, i, 0))
    head_major = jax.ShapeDtypeStruct((nb, n_heads, seq, HEAD_WIDTH), BF16)
    return pl.pallas_call(
        functools.partial(_inproj_kernel, d=d, qk_scale=HEAD_DIM ** -0.5 * math.log2(math.e)),
        grid=(nb, seq // tm),
        in_specs=[tok_spec,
                  pl.BlockSpec((1,) + mod3.shape[1:], lambda b, i: (b, 0, 0)),
                  _resident((d, n_in)),
                  _resident((d, d)),
                  _resident((1, 2 * d)),
                  pl.BlockSpec((tm, LANES), tab),
                  pl.BlockSpec((tm, LANES), tab),
                  pl.BlockSpec((tm, LANES), tab)],
        out_specs=[head_spec, head_spec,
                   pl.BlockSpec((1, n_heads, V_ROWS, tm), lambda b, i: (b, 0, 0, i)),
                   tok_spec, pl.BlockSpec((1, tm, 2 * d), tok)],
        out_shape=[head_major, head_major,
                   jax.ShapeDtypeStruct((nb, n_heads, V_ROWS, seq), BF16),
                   act(d), act(2 * d)],
        compiler_params=_compiler_params(2),
        name="in_projection",
    )(x, mod3, w_in, w_v_t, b_gate.reshape(1, 2 * d), *tables)


ATTN_KEY_CHUNK = 512
ATTN_KEY_ROWS = 16


def _attn_kernel(q_ref, k_ref, vt_ref, lam_ref, g_ref, o_ref,
                 s_a, s_b, p_a, p_b, m_a, m_b, *, tq, out_scale):
    heads, seq = k_ref.shape[1], k_ref.shape[2]
    n_blk = seq // tq
    n_items = heads * n_blk
    key_chunk = min(ATTN_KEY_CHUNK, seq)

    def locate(e):
        if isinstance(e, int):
            return e // n_blk, (e % n_blk) * tq
        return lax.div(e, n_blk), pl.multiple_of(lax.rem(e, n_blk) * tq, tq)

    def fold_rows(t, op):
        acc = t[0:SUBLANES]
        for r0 in range(SUBLANES, t.shape[0], SUBLANES):
            acc = op(acc, t[r0:r0 + SUBLANES])
        return acc

    def all_rows(t, reduce):
        return jnp.broadcast_to(reduce(t, axis=0, keepdims=True), t.shape)

    def stage(score_job, softmax_job, value_job):
        if score_job is not None:
            head, row0 = locate(score_job[0])
            q = q_ref[0, head, pl.ds(row0, tq), :]
            lane = lax.broadcasted_iota(jnp.int32, q.shape, 1)
            zero = jnp.zeros_like(q)
            qq = jnp.concatenate([jnp.where(lane < HEAD_DIM, q, zero),
                                  jnp.where(lane >= HEAD_DIM, q, zero)], axis=0)
        if softmax_job is not None:
            m = softmax_job[1][...]
        m_acc = None
        for c0 in range(0, seq, key_chunk):
            keys = slice(c0, c0 + key_chunk)
            if score_job is not None:
                st = lax.dot_general(k_ref[0, head, keys, :], qq, (((1,), (1,)), ((), ())),
                                     preferred_element_type=F32)
                score_job[1][keys, :] = st
                m_c = fold_rows(st, jnp.maximum)
                m_acc = m_c if m_acc is None else jnp.maximum(m_acc, m_c)
            if softmax_job is not None:
                s_buf, _, p_buf = softmax_job
                for r0 in range(c0, c0 + key_chunk, ATTN_KEY_ROWS):
                    parts = []
                    for r in range(r0, r0 + ATTN_KEY_ROWS, SUBLANES):
                        parts.append(jnp.exp2(s_buf[r:r + SUBLANES, :] - m))
                    p_buf[r0:r0 + ATTN_KEY_ROWS, :] = jnp.concatenate(parts, axis=0).astype(BF16)
        if value_job is not None:
            v_head, v_row0 = locate(value_job[0])
            ot = jnp.dot(vt_ref[0, v_head], value_job[1][...], preferred_element_type=F32)
        if score_job is not None:
            score_job[2][...] = all_rows(m_acc, jnp.max)
        if value_job is not None:
            ot = ot[:HEAD_WIDTH] / ot[HEAD_WIDTH:HEAD_WIDTH + 1]
            lam = jnp.concatenate([lam_ref[...]] * (tq // LANES), axis=1)
            att = ot[:, :tq] - lam * ot[:, tq:]
            ms = jnp.mean(att * att, axis=0, keepdims=True)
            y = att * lax.rsqrt(ms + LN_EPS) * g_ref[...] * out_scale
            o_ref[0, v_head, pl.ds(v_row0, tq), :] = y.T.astype(BF16)

    buf_a = (s_a, m_a, p_a)
    buf_b = (s_b, m_b, p_b)
    stage((0, s_a, m_a), None, None)
    stage((1, s_b, m_b), buf_a, None)

    def pair(ee, carry):
        e = 2 * ee
        stage((e + 2, s_a, m_a), buf_b, (e, p_a))
        stage((e + 3, s_b, m_b), buf_a, (e + 1, p_b))
        return carry

    lax.fori_loop(0, n_items // 2 - 1, pair, 0)
    stage(None, buf_b, (n_items - 2, p_a))
    stage(None, None, (n_items - 1, p_b))


def _attention(q, k, v_t, lam, subln_g, lam_init, tq):
    nb, n_heads, seq, _ = q.shape
    assert (seq // tq) % 2 == 0 and tq % LANES == 0 and seq % ATTN_KEY_ROWS == 0
    heads = _heads_per_step(seq, tq, n_heads)
    group = lambda b, g: (b, g, 0, 0)
    head_spec = pl.BlockSpec((1, heads, seq, HEAD_WIDTH), group)
    gain = jnp.broadcast_to(subln_g.astype(F32)[:, None], (HEAD_WIDTH, tq))
    score_buf = pltpu.VMEM((seq, 2 * tq), F32)
    prob_buf = pltpu.VMEM((seq, 2 * tq), BF16)
    stat_buf = pltpu.VMEM((SUBLANES, 2 * tq), F32)
    return pl.pallas_call(
        functools.partial(_attn_kernel, tq=tq, out_scale=1.0 - lam_init),
        grid=(nb, n_heads // heads),
        in_specs=[head_spec, head_spec,
                  pl.BlockSpec((1, heads, V_ROWS, seq), group),
                  pl.BlockSpec((1, LANES), lambda b, g: (0, 0)),
                  pl.BlockSpec((HEAD_WIDTH, tq), lambda b, g: (0, 0))],
        out_specs=head_spec,
        out_shape=jax.ShapeDtypeStruct((nb, n_heads, seq, HEAD_WIDTH), BF16),
        scratch_shapes=[score_buf, score_buf, prob_buf, prob_buf,
                        stat_buf, stat_buf],
        compiler_params=_compiler_params(2),
        name="diff_attention",
    )(q, k, v_t, lam, gain)


CONV_HALO = 16
CONV_ROWS = 16


def _mix_kernel(u_ref, up_ref, un_ref, att_ref, g_ref, x_ref, mod_ref,
                cw_ref, cb_ref, clg_ref, clb_ref, wpw_ref, bpw_ref, wo_ref,
                l1g_ref, l1b_ref, o_ref, ubuf, cbuf, *, alpha):
    tm, d = u_ref.shape[1], u_ref.shape[2]
    width = cw_ref.shape[0]
    pad = (width - 1) // 2
    i = pl.program_id(1)
    last = pl.num_programs(1) - 1

    def row_major(t):
        return t.astype(F32).reshape(t.shape[0], d // LANES, LANES)

    ubuf[0:CONV_HALO] = jnp.where(i > 0, row_major(up_ref[0]), 0.0)
    ubuf[CONV_HALO:CONV_HALO + tm] = row_major(u_ref[0])
    ubuf[CONV_HALO + tm:] = jnp.where(i < last, row_major(un_ref[0]), 0.0)

    def conv_rows(rb, carry):
        r0 = rb * CONV_ROWS
        acc = jnp.zeros((CONV_ROWS, d // LANES, LANES), F32)
        for j in range(width):
            acc = acc + ubuf[pl.ds(r0 + (CONV_HALO - pad + j), CONV_ROWS)] * cw_ref[j]
        cbuf[pl.ds(r0, CONV_ROWS)] = acc
        return carry

    lax.fori_loop(0, tm // CONV_ROWS, conv_rows, 0)

    conv = cbuf[...].reshape(tm, d) + cb_ref[...]
    y = _norm(conv) * clg_ref[...] + clb_ref[...]
    y = y * jax.nn.sigmoid(y)
    y_b = jnp.dot(y.astype(BF16), wpw_ref[...], preferred_element_type=F32) + bpw_ref[...]
    gates = g_ref[0]
    att = jnp.concatenate([att_ref[0, j] for j in range(att_ref.shape[1])], axis=1)
    merged = gates[:, :d] * att + gates[:, d:] * y_b
    mix = jnp.dot(merged.astype(BF16), wo_ref[...], preferred_element_type=F32)
    g1 = mod_ref[0, 2:3, :]
    z = alpha * x_ref[0] + g1 * mix
    o_ref[0] = _norm(z) * l1g_ref[...] + l1b_ref[...]


def _mixer_output(u, att, gates, x, mod3, conv_w, conv_b, conv_ln_g, conv_ln_b,
                  w_pw2, b_pw2, w_o, ln1_g, ln1_b, alpha, tm):
    nb, seq, d = x.shape
    width = conv_w.shape[0]
    n_heads = att.shape[1]
    assert (width - 1) // 2 < CONV_HALO and tm % CONV_ROWS == 0 and tm % CONV_HALO == 0
    hb = tm // CONV_HALO
    n_halo = seq // CONV_HALO
    tok = lambda b, i: (b, i, 0)
    row = lambda v: v.reshape(1, d).astype(F32)
    vec = pl.BlockSpec((1, d), lambda b, i: (0, 0))
    return pl.pallas_call(
        functools.partial(_mix_kernel, alpha=alpha),
        grid=(nb, seq // tm),
        in_specs=[pl.BlockSpec((1, tm, d), tok),
                  pl.BlockSpec((1, CONV_HALO, d), lambda b, i: (b, jnp.maximum(i * hb - 1, 0), 0)),
                  pl.BlockSpec((1, CONV_HALO, d),
                               lambda b, i: (b, jnp.minimum((i + 1) * hb, n_halo - 1), 0)),
                  pl.BlockSpec((1, n_heads, tm, HEAD_WIDTH), lambda b, i: (b, 0, i, 0)),
                  pl.BlockSpec((1, tm, 2 * d), tok),
                  pl.BlockSpec((1, tm, d), tok),
                  pl.BlockSpec((1,) + mod3.shape[1:], lambda b, i: (b, 0, 0)),
                  _resident((width, d // LANES, LANES)), vec, vec, vec,
                  _resident((d, d)), vec,
                  _resident((d, d)), vec, vec],
        out_specs=pl.BlockSpec((1, tm, d), tok),
        out_shape=jax.ShapeDtypeStruct((nb, seq, d), F32),
        scratch_shapes=[pltpu.VMEM((tm + 2 * CONV_HALO, d // LANES, LANES), F32),
                        pltpu.VMEM((tm, d // LANES, LANES), F32)],
        compiler_params=_compiler_params(2),
        name="mixer_output",
    )(u, u, u, att, gates, x, mod3, conv_w.astype(F32).reshape(width, d // LANES, LANES),
      row(conv_b), row(conv_ln_g),
      row(conv_ln_b), w_pw2, row(b_pw2), w_o, row(ln1_g), row(ln1_b))


FFN_HALO = 8
FFN_CHUNK = 256


def _ffn_kernel(x_ref, xp_ref, xn_ref, mod_ref, wup_ref, fcw_ref, fcb_ref, wdn_ref,
                l2g_ref, l2b_ref, o_ref, gbuf, vbuf, abuf, *, alpha, d_ff):
    tm = x_ref.shape[1]
    i = pl.program_id(1)
    last = pl.num_programs(1) - 1
    sh2 = mod_ref[0, 3:4, :]
    sc2 = mod_ref[0, 4:5, :]
    g2 = mod_ref[0, 5:6, :]

    def modnorm(t):
        return _norm(t) * (1.0 + sc2) + sh2

    x = x_ref[0]
    h_prev = jnp.where(i > 0, modnorm(xp_ref[0]), 0.0)
    h_next = jnp.where(i < last, modnorm(xn_ref[0]), 0.0)
    h_ext = jnp.concatenate([h_prev, modnorm(x), h_next], axis=0).astype(BF16)

    def conv3(buf, c0):
        cols = slice(c0, c0 + FFN_CHUNK)
        up = buf[...]
        rows = up.shape[0]
        main = slice(FFN_HALO, FFN_HALO + tm)
        return (pltpu.roll(up, 1, 0)[main] * fcw_ref[0:1, cols]
                + up[main] * fcw_ref[1:2, cols]
                + pltpu.roll(up, rows - 1, 0)[main] * fcw_ref[2:3, cols]
                + fcb_ref[:, cols])

    for c in range(d_ff // FFN_CHUNK):
        c0 = c * FFN_CHUNK
        gbuf[...] = jnp.dot(h_ext, wup_ref[:, c0:c0 + FFN_CHUNK], preferred_element_type=F32)
        vbuf[...] = jnp.dot(h_ext, wup_ref[:, d_ff + c0:d_ff + c0 + FFN_CHUNK],
                            preferred_element_type=F32)
        fg = conv3(gbuf, c0)
        fv = conv3(vbuf, d_ff + c0)
        gelu = 0.5 * fg * (1.0 + lax.erf(fg * (2.0 ** -0.5)))
        abuf[:, c0:c0 + FFN_CHUNK] = (gelu * fv).astype(BF16)

    f = jnp.dot(abuf[...], wdn_ref[...], preferred_element_type=F32)
    z = alpha * x + g2 * f
    o_ref[0] = _norm(z) * l2g_ref[...] + l2b_ref[...]


def _channel_mixer(x1, mod3, w_up, ffn_conv_w, ffn_conv_b, w_down, ln2_g, ln2_b, alpha, tm):
    nb, seq, d = x1.shape
    d_ff = w_down.shape[0]
    assert d_ff % FFN_CHUNK == 0 and ffn_conv_w.shape[0] == 3 and tm % FFN_HALO == 0
    hb = tm // FFN_HALO
    n_halo = seq // FFN_HALO
    tok = lambda b, i: (b, i, 0)
    vec = pl.BlockSpec((1, d), lambda b, i: (0, 0))
    return pl.pallas_call(
        functools.partial(_ffn_kernel, alpha=alpha, d_ff=d_ff),
        grid=(nb, seq // tm),
        in_specs=[pl.BlockSpec((1, tm, d), tok),
                  pl.BlockSpec((1, FFN_HALO, d), lambda b, i: (b, jnp.maximum(i * hb - 1, 0), 0)),
                  pl.BlockSpec((1, FFN_HALO, d),
                               lambda b, i: (b, jnp.minimum((i + 1) * hb, n_halo - 1), 0)),
                  pl.BlockSpec((1,) + mod3.shape[1:], lambda b, i: (b, 0, 0)),
                  _resident((d, 2 * d_ff)),
                  _resident((3, 2 * d_ff)),
                  _resident((1, 2 * d_ff)),
                  _resident((d_ff, d)), vec, vec],
        out_specs=pl.BlockSpec((1, tm, d), tok),
        out_shape=jax.ShapeDtypeStruct((nb, seq, d), F32),
        scratch_shapes=[pltpu.VMEM((tm + 2 * FFN_HALO, FFN_CHUNK), F32),
                        pltpu.VMEM((tm + 2 * FFN_HALO, FFN_CHUNK), F32),
                        pltpu.VMEM((tm, d_ff), BF16)],
        compiler_params=_compiler_params(2),
        name="channel_mixer",
    )(x1, x1, x1, mod3, w_up, ffn_conv_w.astype(F32), ffn_conv_b.reshape(1, 2 * d_ff).astype(F32),
      w_down, ln2_g.reshape(1, d).astype(F32), ln2_b.reshape(1, d).astype(F32))


def _encoder_layer(x, mod3, lam, lam_init, alpha, w_in, w_v_t, b_gate, subln_g, conv_w, conv_b,
                   conv_ln_g, conv_ln_b, w_pw2, b_pw2, w_o, ln1_g, ln1_b, w_up, ffn_conv_w,
                   ffn_conv_b, w_down, ln2_g, ln2_b):
    seq = x.shape[1]
    tm, tq = _tiles(seq)
    q, k, v_t, u, gates = _in_projection(x, mod3, w_in, w_v_t, b_gate, _rope_tables(seq), tm)
    att = _attention(q, k, v_t, lam, subln_g, lam_init, tq)
    x1 = _mixer_output(u, att, gates, x, mod3, conv_w, conv_b, conv_ln_g, conv_ln_b,
                       w_pw2, b_pw2, w_o, ln1_g, ln1_b, alpha, tm)
    return _channel_mixer(x1, mod3, w_up, ffn_conv_w, ffn_conv_b, w_down, ln2_g, ln2_b, alpha, tm)


def kernel(x_prompt, x_sample, c_prompt, c_sample, w_ada, b_ada, w_in, b_gate, lambda_q1, lambda_k1, lambda_q2, lambda_k2, subln_g, conv_w, conv_b, conv_ln_g, conv_ln_b, w_pw2, b_pw2, w_o, ln1_g, ln1_b, w_up, ffn_conv_w, ffn_conv_b, w_down, ln2_g, ln2_b):
    depth = w_ada.shape[0]
    d = x_prompt.shape[-1]
    alpha = (2.0 * depth) ** 0.25
    n_prompt = c_prompt.shape[0]
    y_prompt, y_sample = x_prompt, x_sample
    for l in range(depth):
        lam_init = 0.8 - 0.6 * math.exp(-0.3 * l)
        mod = _modulation(jnp.concatenate([c_prompt, c_sample], axis=0), w_ada[l], b_ada[l])
        mod3 = mod.reshape(mod.shape[0], -1, d)
        lam = _lambda(lambda_q1[l], lambda_k1[l], lambda_q2[l], lambda_k2[l], lam_init)
        w_v_t = w_in[l][:, 2 * d:3 * d].T.astype(BF16)
        weights = (w_in[l].astype(BF16), w_v_t, b_gate[l], subln_g[l], conv_w[l], conv_b[l],
                   conv_ln_g[l], conv_ln_b[l], w_pw2[l].astype(BF16), b_pw2[l],
                   w_o[l].astype(BF16), ln1_g[l], ln1_b[l], w_up[l].astype(BF16),
                   ffn_conv_w[l], ffn_conv_b[l], w_down[l].astype(BF16), ln2_g[l], ln2_b[l])
        y_prompt = _encoder_layer(y_prompt, mod3[:n_prompt], lam, lam_init, alpha, *weights)
        y_sample = _encoder_layer(y_sample, mod3[n_prompt:], lam, lam_init, alpha, *weights)
    return (y_prompt, y_sample)
```

```python
import functools
import math

import jax
import jax.numpy as jnp
from jax import lax
from jax.experimental import pallas as pl
from jax.experimental.pallas import tpu as pltpu

F32 = jnp.float32
BF16 = jnp.bfloat16

HEAD_DIM = 64
HEAD_WIDTH = 2 * HEAD_DIM
ROT_DIM = HEAD_DIM // 4
ROPE_THETA = 500000.0
LN_EPS = 1e-5
LANES = 128
SUBLANES = 8
SUM_ROWS = 16
V_ROWS = HEAD_WIDTH + SUM_ROWS
VMEM_LIMIT_BYTES = 56 * 1024 * 1024
ATTN_VMEM_BUDGET = 48 * 1024 * 1024


def _compiler_params(n_grid_dims):
    return pltpu.CompilerParams(
        dimension_semantics=("arbitrary",) * n_grid_dims,
        vmem_limit_bytes=VMEM_LIMIT_BYTES)


def _resident(shape):
    zeros = (0,) * len(shape)
    return pl.BlockSpec(shape, lambda *_: zeros, pipeline_mode=pl.Buffered(1))


def _tiles(seq):
    tm = min(512, seq)
    tq = min(256, seq // 2)
    return tm, tq


def _heads_per_step(seq, tq, n_heads):
    scratch = 2 * seq * 2 * tq * (4 + 2)
    per_head = 2 * seq * (3 * HEAD_WIDTH + V_ROWS) * 2
    heads = n_heads
    while heads > 1 and scratch + heads * per_head > ATTN_VMEM_BUDGET:
        heads //= 2
    return heads


def _norm(x):
    mu = jnp.mean(x, axis=-1, keepdims=True)
    xc = x - mu
    var = jnp.mean(xc * xc, axis=-1, keepdims=True)
    return xc * lax.rsqrt(var + LN_EPS)


def _mod_kernel(c_ref, w_ref, b_ref, o_ref):
    c = c_ref[...]
    a = c * jax.nn.sigmoid(c)
    o_ref[...] = jnp.dot(a, w_ref[...], preferred_element_type=F32,
                         precision=lax.Precision.HIGHEST) + b_ref[...]


def _modulation(c, w_ada, b_ada):
    nb, d = c.shape
    n_out = w_ada.shape[1]
    return pl.pallas_call(
        _mod_kernel,
        grid=(n_out // d,),
        in_specs=[pl.BlockSpec((nb, d), lambda j: (0, 0)),
                  pl.BlockSpec((d, d), lambda j: (0, j)),
                  pl.BlockSpec((1, d), lambda j: (0, j))],
        out_specs=pl.BlockSpec((nb, d), lambda j: (0, j)),
        out_shape=jax.ShapeDtypeStruct((nb, n_out), F32),
        compiler_params=_compiler_params(1),
        name="adaln_modulation",
    )(c, w_ada, b_ada.reshape(1, n_out))


def _lambda_kernel(l_ref, o_ref, *, lam_init):
    a = jnp.sum(l_ref[0:1, :] * l_ref[1:2, :], axis=-1, keepdims=True)
    b = jnp.sum(l_ref[2:3, :] * l_ref[3:4, :], axis=-1, keepdims=True)
    lam = jnp.exp(a) - jnp.exp(b) + lam_init
    o_ref[...] = jnp.broadcast_to(lam, o_ref.shape)


def _lambda(lq1, lk1, lq2, lk2, lam_init):
    stacked = jnp.stack([lq1, lk1, lq2, lk2]).astype(F32)
    return pl.pallas_call(
        functools.partial(_lambda_kernel, lam_init=lam_init),
        out_shape=jax.ShapeDtypeStruct((1, LANES), F32),
        name="diff_lambda",
    )(stacked)


def _rope_tables(seq):
    half = ROT_DIM // 2
    inv_freq = ROPE_THETA ** (-jnp.arange(0, ROT_DIM, 2, dtype=F32) / ROT_DIM)
    ang = jnp.arange(seq, dtype=F32)[:, None] * inv_freq[None, :]
    cos, sin = jnp.cos(ang), jnp.sin(ang)
    ones = jnp.ones((seq, HEAD_DIM - ROT_DIM), F32)
    zeros = jnp.zeros((seq, HEAD_DIM - ROT_DIM), F32)
    zh = jnp.zeros((seq, half), F32)
    cos_t = jnp.concatenate([cos, cos, ones], axis=1)
    sin_a = jnp.concatenate([-sin, zh, zeros], axis=1)
    sin_b = jnp.concatenate([zh, sin, zeros], axis=1)
    rep = LANES // HEAD_DIM
    return tuple(jnp.tile(t, (1, rep)) for t in (cos_t, sin_a, sin_b))


def _inproj_kernel(x_ref, mod_ref, w_ref, wvt_ref, bg_ref, cos_ref, sa_ref, sb_ref,
                   q_ref, k_ref, vt_ref, u_ref, g_ref, *, d, qk_scale):
    half = ROT_DIM // 2
    sh1 = mod_ref[0, 0:1, :]
    sc1 = mod_ref[0, 1:2, :]
    h = (_norm(x_ref[0]) * (1.0 + sc1) + sh1).astype(BF16)

    def proj(c0, width):
        return jnp.dot(h, w_ref[:, c0:c0 + width], preferred_element_type=F32)

    cos_t, sin_a, sin_b = cos_ref[...], sa_ref[...], sb_ref[...]

    def store_rope(t, o_ref, scale):
        for j in range(d // HEAD_WIDTH):
            s = t[:, j * HEAD_WIDTH:(j + 1) * HEAD_WIDTH]
            r = (s * cos_t + pltpu.roll(s, LANES - half, 1) * sin_a
                 + pltpu.roll(s, half, 1) * sin_b)
            if scale != 1.0:
                r = r * scale
            o_ref[0, j] = r.astype(BF16)

    store_rope(proj(0, d), q_ref, qk_scale)
    store_rope(proj(d, d), k_ref, 1.0)
    v_t = lax.dot_general(wvt_ref[...], h, (((1,), (1,)), ((), ())),
                          preferred_element_type=F32).astype(BF16)
    n_heads = d // HEAD_WIDTH
    vt_ref[0, :, 0:HEAD_WIDTH, :] = v_t.reshape(n_heads, HEAD_WIDTH, v_t.shape[1])
    vt_ref[0, :, HEAD_WIDTH:, :] = jnp.ones((n_heads, SUM_ROWS, v_t.shape[1]), BF16)
    glu_a = proj(3 * d, d)
    glu_b = proj(4 * d, d)
    u_ref[0] = (glu_a * jax.nn.sigmoid(glu_b)).reshape(u_ref.shape[1:])
    gate = proj(5 * d, 2 * d) + bg_ref[...]
    g_ref[0] = jax.nn.sigmoid(gate).astype(BF16)


def _in_projection(x, mod3, w_in, w_v_t, b_gate, tables, tm):
    nb, seq, d = x.shape
    n_in = w_in.shape[1]
    n_heads = d // HEAD_WIDTH
    assert HEAD_WIDTH == LANES
    tok = lambda b, i: (b, i, 0)
    tab = lambda b, i: (i, 0)
    act = lambda width: jax.ShapeDtypeStruct((nb, seq, width), BF16)
    tok_spec = pl.BlockSpec((1, tm, d), tok)
    head_spec = pl.BlockSpec((1, n_heads, tm, HEAD_WIDTH), lambda b, i: (b, 0, i, 0))
    head_major = jax.ShapeDtypeStruct((nb, n_heads, seq, HEAD_WIDTH), BF16)
    return pl.pallas_call(
        functools.partial(_inproj_kernel, d=d, qk_scale=HEAD_DIM ** -0.5 * math.log2(math.e)),
        grid=(nb, seq // tm),
        in_specs=[tok_spec,
                  pl.BlockSpec((1,) + mod3.shape[1:], lambda b, i: (b, 0, 0)),
                  _resident((d, n_in)),
                  _resident((d, d)),
                  _resident((1, 2 * d)),
                  pl.BlockSpec((tm, LANES), tab),
                  pl.BlockSpec((tm, LANES), tab),
                  pl.BlockSpec((tm, LANES), tab)],
        out_specs=[head_spec, head_spec,
                   pl.BlockSpec((1, n_heads, V_ROWS, tm), lambda b, i: (b, 0, 0, i)),
                   pl.BlockSpec((1, tm, d // LANES, LANES), lambda b, i: (b, i, 0, 0)),
                   pl.BlockSpec((1, tm, 2 * d), tok)],
        out_shape=[head_major, head_major,
                   jax.ShapeDtypeStruct((nb, n_heads, V_ROWS, seq), BF16),
                   jax.ShapeDtypeStruct((nb, seq, d // LANES, LANES), F32),
                   act(2 * d)],
        compiler_params=_compiler_params(2),
        name="in_projection",
    )(x, mod3, w_in, w_v_t, b_gate.reshape(1, 2 * d), *tables)


ATTN_KEY_CHUNK = 512
ATTN_KEY_ROWS = 16


def _attn_kernel(q_ref, k_ref, vt_ref, lam_ref, g_ref, o_ref,
                 s_a, s_b, p_a, p_b, m_a, m_b, *, tq, out_scale):
    heads, seq = k_ref.shape[1], k_ref.shape[2]
    n_blk = seq // tq
    n_items = heads * n_blk
    key_chunk = min(ATTN_KEY_CHUNK, seq)

    def locate(e):
        if isinstance(e, int):
            return e // n_blk, (e % n_blk) * tq
        return lax.div(e, n_blk), pl.multiple_of(lax.rem(e, n_blk) * tq, tq)

    def fold_rows(t, op):
        acc = t[0:SUBLANES]
        for r0 in range(SUBLANES, t.shape[0], SUBLANES):
            acc = op(acc, t[r0:r0 + SUBLANES])
        return acc

    def all_rows(t, reduce):
        return jnp.broadcast_to(reduce(t, axis=0, keepdims=True), t.shape)

    def stage(score_job, softmax_job, value_job):
        if score_job is not None:
            head, row0 = locate(score_job[0])
            q = q_ref[0, head, pl.ds(row0, tq), :]
            lane = lax.broadcasted_iota(jnp.int32, q.shape, 1)
            zero = jnp.zeros_like(q)
            qq = jnp.concatenate([jnp.where(lane < HEAD_DIM, q, zero),
                                  jnp.where(lane >= HEAD_DIM, q, zero)], axis=0)
        if softmax_job is not None:
            m = softmax_job[1][...]
        m_acc = None
        for c0 in range(0, seq, key_chunk):
            keys = slice(c0, c0 + key_chunk)
            if score_job is not None:
                st = lax.dot_general(k_ref[0, head, keys, :], qq, (((1,), (1,)), ((), ())),
                                     preferred_element_type=F32)
                score_job[1][keys, :] = st
                m_c = fold_rows(st, jnp.maximum)
                m_acc = m_c if m_acc is None else jnp.maximum(m_acc, m_c)
            if softmax_job is not None:
                s_buf, _, p_buf = softmax_job
                for r0 in range(c0, c0 + key_chunk, ATTN_KEY_ROWS):
                    parts = []
                    for r in range(r0, r0 + ATTN_KEY_ROWS, SUBLANES):
                        parts.append(jnp.exp2(s_buf[r:r + SUBLANES, :] - m))
                    p_buf[r0:r0 + ATTN_KEY_ROWS, :] = jnp.concatenate(parts, axis=0).astype(BF16)
        if value_job is not None:
            v_head, v_row0 = locate(value_job[0])
            ot = jnp.dot(vt_ref[0, v_head], value_job[1][...], preferred_element_type=F32)
        if score_job is not None:
            score_job[2][...] = all_rows(m_acc, jnp.max)
        if value_job is not None:
            ot = ot[:HEAD_WIDTH] / ot[HEAD_WIDTH:HEAD_WIDTH + 1]
            lam = jnp.concatenate([lam_ref[...]] * (tq // LANES), axis=1)
            att = ot[:, :tq] - lam * ot[:, tq:]
            ms = jnp.mean(att * att, axis=0, keepdims=True)
            y = att * lax.rsqrt(ms + LN_EPS) * g_ref[...] * out_scale
            o_ref[0, v_head, pl.ds(v_row0, tq), :] = y.T.astype(BF16)

    buf_a = (s_a, m_a, p_a)
    buf_b = (s_b, m_b, p_b)
    stage((0, s_a, m_a), None, None)
    stage((1, s_b, m_b), buf_a, None)

    def pair(ee, carry):
        e = 2 * ee
        stage((e + 2, s_a, m_a), buf_b, (e, p_a))
        stage((e + 3, s_b, m_b), buf_a, (e + 1, p_b))
        return carry

    lax.fori_loop(0, n_items // 2 - 1, pair, 0)
    stage(None, buf_b, (n_items - 2, p_a))
    stage(None, None, (n_items - 1, p_b))


def _attention(q, k, v_t, lam, subln_g, lam_init, tq):
    nb, n_heads, seq, _ = q.shape
    assert (seq // tq) % 2 == 0 and tq % LANES == 0 and seq % ATTN_KEY_ROWS == 0
    heads = _heads_per_step(seq, tq, n_heads)
    group = lambda b, g: (b, g, 0, 0)
    head_spec = pl.BlockSpec((1, heads, seq, HEAD_WIDTH), group)
    gain = jnp.broadcast_to(subln_g.astype(F32)[:, None], (HEAD_WIDTH, tq))
    score_buf = pltpu.VMEM((seq, 2 * tq), F32)
    prob_buf = pltpu.VMEM((seq, 2 * tq), BF16)
    stat_buf = pltpu.VMEM((SUBLANES, 2 * tq), F32)
    return pl.pallas_call(
        functools.partial(_attn_kernel, tq=tq, out_scale=1.0 - lam_init),
        grid=(nb, n_heads // heads),
        in_specs=[head_spec, head_spec,
                  pl.BlockSpec((1, heads, V_ROWS, seq), group),
                  pl.BlockSpec((1, LANES), lambda b, g: (0, 0)),
                  pl.BlockSpec((HEAD_WIDTH, tq), lambda b, g: (0, 0))],
        out_specs=head_spec,
        out_shape=jax.ShapeDtypeStruct((nb, n_heads, seq, HEAD_WIDTH), BF16),
        scratch_shapes=[score_buf, score_buf, prob_buf, prob_buf,
                        stat_buf, stat_buf],
        compiler_params=_compiler_params(2),
        name="diff_attention",
    )(q, k, v_t, lam, gain)


CONV_HALO = 16
CONV_ROWS = 32


def _mix_kernel(u_ref, up_ref, un_ref, att_ref, g_ref, x_ref, mod_ref,
                cw_ref, cb_ref, clg_ref, clb_ref, wpw_ref, bpw_ref, wo_ref,
                l1g_ref, l1b_ref, o_ref, ubuf, cbuf, *, alpha):
    tm, d = x_ref.shape[1], x_ref.shape[2]
    width = cw_ref.shape[0]
    pad = (width - 1) // 2
    i = pl.program_id(1)
    last = pl.num_programs(1) - 1

    ubuf[0:CONV_HALO] = jnp.where(i > 0, up_ref[0], 0.0)
    ubuf[CONV_HALO:CONV_HALO + tm] = u_ref[0]
    ubuf[CONV_HALO + tm:] = jnp.where(i < last, un_ref[0], 0.0)

    def conv_rows(rb, carry):
        r0 = rb * CONV_ROWS
        acc = jnp.zeros((CONV_ROWS, d // LANES, LANES), F32)
        for j in range(width):
            acc = acc + ubuf[pl.ds(r0 + (CONV_HALO - pad + j), CONV_ROWS)] * cw_ref[j]
        cbuf[pl.ds(r0, CONV_ROWS)] = acc
        return carry

    lax.fori_loop(0, tm // CONV_ROWS, conv_rows, 0)

    conv = cbuf[...].reshape(tm, d) + cb_ref[...]
    y = _norm(conv) * clg_ref[...] + clb_ref[...]
    y = y * jax.nn.sigmoid(y)
    y_b = jnp.dot(y.astype(BF16), wpw_ref[...], preferred_element_type=F32) + bpw_ref[...]
    gates = g_ref[0]
    att = jnp.concatenate([att_ref[0, j] for j in range(att_ref.shape[1])], axis=1)
    merged = gates[:, :d] * att + gates[:, d:] * y_b
    mix = jnp.dot(merged.astype(BF16), wo_ref[...], preferred_element_type=F32)
    g1 = mod_ref[0, 2:3, :]
    z = alpha * x_ref[0] + g1 * mix
    o_ref[0] = _norm(z) * l1g_ref[...] + l1b_ref[...]


def _mixer_output(u, att, gates, x, mod3, conv_w, conv_b, conv_ln_g, conv_ln_b,
                  w_pw2, b_pw2, w_o, ln1_g, ln1_b, alpha, tm):
    nb, seq, d = x.shape
    width = conv_w.shape[0]
    n_heads = att.shape[1]
    assert (width - 1) // 2 < CONV_HALO and tm % CONV_ROWS == 0 and tm % CONV_HALO == 0
    hb = tm // CONV_HALO
    n_halo = seq // CONV_HALO
    rows3 = (d // LANES, LANES)
    tok = lambda b, i: (b, i, 0)
    row = lambda v: v.reshape(1, d).astype(F32)
    vec = pl.BlockSpec((1, d), lambda b, i: (0, 0))
    return pl.pallas_call(
        functools.partial(_mix_kernel, alpha=alpha),
        grid=(nb, seq // tm),
        in_specs=[pl.BlockSpec((1, tm) + rows3, lambda b, i: (b, i, 0, 0)),
                  pl.BlockSpec((1, CONV_HALO) + rows3,
                               lambda b, i: (b, jnp.maximum(i * hb - 1, 0), 0, 0)),
                  pl.BlockSpec((1, CONV_HALO) + rows3,
                               lambda b, i: (b, jnp.minimum((i + 1) * hb, n_halo - 1), 0, 0)),
                  pl.BlockSpec((1, n_heads, tm, HEAD_WIDTH), lambda b, i: (b, 0, i, 0)),
                  pl.BlockSpec((1, tm, 2 * d), tok),
                  pl.BlockSpec((1, tm, d), tok),
                  pl.BlockSpec((1,) + mod3.shape[1:], lambda b, i: (b, 0, 0)),
                  _resident((width, d // LANES, LANES)), vec, vec, vec,
                  _resident((d, d)), vec,
                  _resident((d, d)), vec, vec],
        out_specs=pl.BlockSpec((1, tm, d), tok),
        out_shape=jax.ShapeDtypeStruct((nb, seq, d), F32),
        scratch_shapes=[pltpu.VMEM((tm + 2 * CONV_HALO, d // LANES, LANES), F32),
                        pltpu.VMEM((tm, d // LANES, LANES), F32)],
        compiler_params=_compiler_params(2),
        name="mixer_output",
    )(u, u, u, att, gates, x, mod3, conv_w.astype(F32).reshape(width, d // LANES, LANES),
      row(conv_b), row(conv_ln_g),
      row(conv_ln_b), w_pw2, row(b_pw2), w_o, row(ln1_g), row(ln1_b))


FFN_HALO = 8
FFN_CHUNK = 256
GELU_FOLD = 2.0 ** -0.5


def _ffn_kernel(x_ref, xp_ref, xn_ref, mod_ref, wup_ref, fcw_ref, fcb_ref, wdn_ref,
                l2g_ref, l2b_ref, o_ref, gbuf, vbuf, abuf, *, alpha, d_ff):
    tm = x_ref.shape[1]
    i = pl.program_id(1)
    last = pl.num_programs(1) - 1
    sh2 = mod_ref[0, 3:4, :]
    sc2 = mod_ref[0, 4:5, :]
    g2 = mod_ref[0, 5:6, :]

    def modnorm(t):
        return _norm(t) * (1.0 + sc2) + sh2

    x = x_ref[0]
    h_prev = jnp.where(i > 0, modnorm(xp_ref[0]), 0.0)
    h_next = jnp.where(i < last, modnorm(xn_ref[0]), 0.0)
    h_ext = jnp.concatenate([h_prev, modnorm(x), h_next], axis=0).astype(BF16)

    def conv3(buf, c0):
        cols = slice(c0, c0 + FFN_CHUNK)
        up = buf[...]
        rows = up.shape[0]
        main = slice(FFN_HALO, FFN_HALO + tm)
        return (pltpu.roll(up, 1, 0)[main] * fcw_ref[0:1, cols]
                + up[main] * fcw_ref[1:2, cols]
                + pltpu.roll(up, rows - 1, 0)[main] * fcw_ref[2:3, cols]
                + fcb_ref[:, cols])

    for c in range(d_ff // FFN_CHUNK):
        c0 = c * FFN_CHUNK
        gbuf[...] = jnp.dot(h_ext, wup_ref[:, c0:c0 + FFN_CHUNK], preferred_element_type=F32)
        vbuf[...] = jnp.dot(h_ext, wup_ref[:, d_ff + c0:d_ff + c0 + FFN_CHUNK],
                            preferred_element_type=F32)
        fg = conv3(gbuf, c0)
        fv = conv3(vbuf, d_ff + c0)
        abuf[:, c0:c0 + FFN_CHUNK] = (fg * fv * (1.0 + lax.erf(fg))).astype(BF16)

    f = jnp.dot(abuf[...], wdn_ref[...], preferred_element_type=F32)
    z = alpha * x + g2 * f
    o_ref[0] = _norm(z) * l2g_ref[...] + l2b_ref[...]


def _channel_mixer(x1, mod3, w_up, ffn_conv_w, ffn_conv_b, w_down, ln2_g, ln2_b, alpha, tm):
    nb, seq, d = x1.shape
    d_ff = w_down.shape[0]
    assert d_ff % FFN_CHUNK == 0 and ffn_conv_w.shape[0] == 3 and tm % FFN_HALO == 0
    hb = tm // FFN_HALO
    n_halo = seq // FFN_HALO
    tok = lambda b, i: (b, i, 0)
    vec = pl.BlockSpec((1, d), lambda b, i: (0, 0))
    return pl.pallas_call(
        functools.partial(_ffn_kernel, alpha=alpha, d_ff=d_ff),
        grid=(nb, seq // tm),
        in_specs=[pl.BlockSpec((1, tm, d), tok),
                  pl.BlockSpec((1, FFN_HALO, d), lambda b, i: (b, jnp.maximum(i * hb - 1, 0), 0)),
                  pl.BlockSpec((1, FFN_HALO, d),
                               lambda b, i: (b, jnp.minimum((i + 1) * hb, n_halo - 1), 0)),
                  pl.BlockSpec((1,) + mod3.shape[1:], lambda b, i: (b, 0, 0)),
                  _resident((d, 2 * d_ff)),
                  _resident((3, 2 * d_ff)),
                  _resident((1, 2 * d_ff)),
                  _resident((d_ff, d)), vec, vec],
        out_specs=pl.BlockSpec((1, tm, d), tok),
        out_shape=jax.ShapeDtypeStruct((nb, seq, d), F32),
        scratch_shapes=[pltpu.VMEM((tm + 2 * FFN_HALO, FFN_CHUNK), F32),
                        pltpu.VMEM((tm + 2 * FFN_HALO, FFN_CHUNK), F32),
                        pltpu.VMEM((tm, d_ff), BF16)],
        compiler_params=_compiler_params(2),
        name="channel_mixer",
    )(x1, x1, x1, mod3, w_up, ffn_conv_w.astype(F32) * GELU_FOLD,
      ffn_conv_b.reshape(1, 2 * d_ff).astype(F32) * GELU_FOLD,
      w_down, ln2_g.reshape(1, d).astype(F32), ln2_b.reshape(1, d).astype(F32))


def _encoder_layer(x, mod3, lam, lam_init, alpha, w_in, w_v_t, b_gate, subln_g, conv_w, conv_b,
                   conv_ln_g, conv_ln_b, w_pw2, b_pw2, w_o, ln1_g, ln1_b, w_up, ffn_conv_w,
                   ffn_conv_b, w_down, ln2_g, ln2_b):
    seq = x.shape[1]
    tm, tq = _tiles(seq)
    q, k, v_t, u, gates = _in_projection(x, mod3, w_in, w_v_t, b_gate, _rope_tables(seq), tm)
    att = _attention(q, k, v_t, lam, subln_g, lam_init, tq)
    x1 = _mixer_output(u, att, gates, x, mod3, conv_w, conv_b, conv_ln_g, conv_ln_b,
                       w_pw2, b_pw2, w_o, ln1_g, ln1_b, alpha, tm)
    return _channel_mixer(x1, mod3, w_up, ffn_conv_w, ffn_conv_b, w_down, ln2_g, ln2_b, alpha, tm)


def kernel(x_prompt, x_sample, c_prompt, c_sample, w_ada, b_ada, w_in, b_gate, lambda_q1, lambda_k1, lambda_q2, lambda_k2, subln_g, conv_w, conv_b, conv_ln_g, conv_ln_b, w_pw2, b_pw2, w_o, ln1_g, ln1_b, w_up, ffn_conv_w, ffn_conv_b, w_down, ln2_g, ln2_b):
    depth = w_ada.shape[0]
    d = x_prompt.shape[-1]
    alpha = (2.0 * depth) ** 0.25
    n_prompt = c_prompt.shape[0]
    y_prompt, y_sample = x_prompt, x_sample
    for l in range(depth):
        lam_init = 0.8 - 0.6 * math.exp(-0.3 * l)
        mod = _modulation(jnp.concatenate([c_prompt, c_sample], axis=0), w_ada[l], b_ada[l])
        mod3 = mod.reshape(mod.shape[0], -1, d)
        lam = _lambda(lambda_q1[l], lambda_k1[l], lambda_q2[l], lambda_k2[l], lam_init)
        w_v_t = w_in[l][:, 2 * d:3 * d].T.astype(BF16)
        weights = (w_in[l].astype(BF16), w_v_t, b_gate[l], subln_g[l], conv_w[l], conv_b[l],
                   conv_ln_g[l], conv_ln_b[l], w_pw2[l].astype(BF16), b_pw2[l],
                   w_o[l].astype(BF16), ln1_g[l], ln1_b[l], w_up[l].astype(BF16),
                   ffn_conv_w[l], ffn_conv_b[l], w_down[l].astype(BF16), ln2_g[l], ln2_b[l])
        y_prompt = _encoder_layer(y_prompt, mod3[:n_prompt], lam, lam_init, alpha, *weights)
        y_sample = _encoder_layer(y_sample, mod3[n_prompt:], lam, lam_init, alpha, *weights)
    return (y_prompt, y_sample)
```

```python
import functools
import math

import jax
import jax.numpy as jnp
from jax import lax
from jax.experimental import pallas as pl
from jax.experimental.pallas import tpu as pltpu

F32 = jnp.float32
BF16 = jnp.bfloat16

HEAD_DIM = 64
HEAD_WIDTH = 2 * HEAD_DIM
ROT_DIM = HEAD_DIM // 4
ROPE_THETA = 500000.0
LN_EPS = 1e-5
LANES = 128
SUBLANES = 8
SUM_ROWS = 16
V_ROWS = HEAD_WIDTH + SUM_ROWS
VMEM_LIMIT_BYTES = 56 * 1024 * 1024
ATTN_VMEM_BUDGET = 48 * 1024 * 1024
SCORE_BUFFER_BYTES = 8 * 1024 * 1024


def _compiler_params(n_grid_dims):
    return pltpu.CompilerParams(
        dimension_semantics=("arbitrary",) * n_grid_dims,
        vmem_limit_bytes=VMEM_LIMIT_BYTES)


def _resident(shape):
    zeros = (0,) * len(shape)
    return pl.BlockSpec(shape, lambda *_: zeros, pipeline_mode=pl.Buffered(1))


def _tiles(seq):
    tm = min(512, seq)
    tq = LANES
    while 2 * tq <= seq // 2 and seq * 4 * tq * 4 <= SCORE_BUFFER_BYTES:
        tq *= 2
    return tm, tq


def _heads_per_step(seq, tq, n_heads):
    scratch = 2 * seq * 2 * tq * (4 + 2)
    per_head = 2 * seq * (3 * HEAD_WIDTH + V_ROWS) * 2
    heads = n_heads
    while heads > 1 and scratch + heads * per_head > ATTN_VMEM_BUDGET:
        heads //= 2
    return heads


def _norm(x):
    mu = jnp.mean(x, axis=-1, keepdims=True)
    xc = x - mu
    var = jnp.mean(xc * xc, axis=-1, keepdims=True)
    return xc * lax.rsqrt(var + LN_EPS)


def _mod_kernel(c_ref, w_ref, b_ref, o_ref):
    c = c_ref[...]
    a = c * jax.nn.sigmoid(c)
    o_ref[...] = jnp.dot(a, w_ref[...], preferred_element_type=F32,
                         precision=lax.Precision.HIGHEST) + b_ref[...]


def _modulation(c, w_ada, b_ada):
    nb, d = c.shape
    n_out = w_ada.shape[1]
    return pl.pallas_call(
        _mod_kernel,
        grid=(n_out // d,),
        in_specs=[pl.BlockSpec((nb, d), lambda j: (0, 0)),
                  pl.BlockSpec((d, d), lambda j: (0, j)),
                  pl.BlockSpec((1, d), lambda j: (0, j))],
        out_specs=pl.BlockSpec((nb, d), lambda j: (0, j)),
        out_shape=jax.ShapeDtypeStruct((nb, n_out), F32),
        compiler_params=_compiler_params(1),
        name="adaln_modulation",
    )(c, w_ada, b_ada.reshape(1, n_out))


def _lambda_kernel(l_ref, o_ref, *, lam_init):
    a = jnp.sum(l_ref[0:1, :] * l_ref[1:2, :], axis=-1, keepdims=True)
    b = jnp.sum(l_ref[2:3, :] * l_ref[3:4, :], axis=-1, keepdims=True)
    lam = jnp.exp(a) - jnp.exp(b) + lam_init
    o_ref[...] = jnp.broadcast_to(lam, o_ref.shape)


def _lambda(lq1, lk1, lq2, lk2, lam_init):
    stacked = jnp.stack([lq1, lk1, lq2, lk2]).astype(F32)
    return pl.pallas_call(
        functools.partial(_lambda_kernel, lam_init=lam_init),
        out_shape=jax.ShapeDtypeStruct((1, LANES), F32),
        name="diff_lambda",
    )(stacked)


def _rope_tables(seq):
    half = ROT_DIM // 2
    inv_freq = ROPE_THETA ** (-jnp.arange(0, ROT_DIM, 2, dtype=F32) / ROT_DIM)
    ang = jnp.arange(seq, dtype=F32)[:, None] * inv_freq[None, :]
    cos, sin = jnp.cos(ang), jnp.sin(ang)
    ones = jnp.ones((seq, HEAD_DIM - ROT_DIM), F32)
    zeros = jnp.zeros((seq, HEAD_DIM - ROT_DIM), F32)
    zh = jnp.zeros((seq, half), F32)
    cos_t = jnp.concatenate([cos, cos, ones], axis=1)
    sin_a = jnp.concatenate([-sin, zh, zeros], axis=1)
    sin_b = jnp.concatenate([zh, sin, zeros], axis=1)
    rep = LANES // HEAD_DIM
    return tuple(jnp.tile(t, (1, rep)) for t in (cos_t, sin_a, sin_b))


def _inproj_kernel(x_ref, mod_ref, w_ref, wvt_ref, bg_ref, cos_ref, sa_ref, sb_ref,
                   q_ref, k_ref, vt_ref, u_ref, g_ref, *, d, qk_scale):
    half = ROT_DIM // 2
    sh1 = mod_ref[0, 0:1, :]
    sc1 = mod_ref[0, 1:2, :]
    h = (_norm(x_ref[0]) * (1.0 + sc1) + sh1).astype(BF16)

    def proj(c0, width):
        return jnp.dot(h, w_ref[:, c0:c0 + width], preferred_element_type=F32)

    cos_t, sin_a, sin_b = cos_ref[...], sa_ref[...], sb_ref[...]

    def store_rope(t, o_ref, scale):
        for j in range(d // HEAD_WIDTH):
            s = t[:, j * HEAD_WIDTH:(j + 1) * HEAD_WIDTH]
            r = (s * cos_t + pltpu.roll(s, LANES - half, 1) * sin_a
                 + pltpu.roll(s, half, 1) * sin_b)
            if scale != 1.0:
                r = r * scale
            o_ref[0, j] = r.astype(BF16)

    store_rope(proj(0, d), q_ref, qk_scale)
    store_rope(proj(d, d), k_ref, 1.0)
    v_t = lax.dot_general(wvt_ref[...], h, (((1,), (1,)), ((), ())),
                          preferred_element_type=F32).astype(BF16)
    n_heads = d // HEAD_WIDTH
    vt_ref[0, :, 0:HEAD_WIDTH, :] = v_t.reshape(n_heads, HEAD_WIDTH, v_t.shape[1])
    vt_ref[0, :, HEAD_WIDTH:, :] = jnp.ones((n_heads, SUM_ROWS, v_t.shape[1]), BF16)
    glu_a = proj(3 * d, d)
    glu_b = proj(4 * d, d)
    u_ref[0] = (glu_a * jax.nn.sigmoid(glu_b)).reshape(u_ref.shape[1:])
    gate = proj(5 * d, 2 * d) + bg_ref[...]
    g_ref[0] = jax.nn.sigmoid(gate).astype(BF16)


def _in_projection(x, mod3, w_in, w_v_t, b_gate, tables, tm):
    nb, seq, d = x.shape
    n_in = w_in.shape[1]
    n_heads = d // HEAD_WIDTH
    assert HEAD_WIDTH == LANES
    tok = lambda b, i: (b, i, 0)
    tab = lambda b, i: (i, 0)
    act = lambda width: jax.ShapeDtypeStruct((nb, seq, width), BF16)
    tok_spec = pl.BlockSpec((1, tm, d), tok)
    head_spec = pl.BlockSpec((1, n_heads, tm, HEAD_WIDTH), lambda b, i: (b, 0, i, 0))
    head_major = jax.ShapeDtypeStruct((nb, n_heads, seq, HEAD_WIDTH), BF16)
    return pl.pallas_call(
        functools.partial(_inproj_kernel, d=d, qk_scale=HEAD_DIM ** -0.5 * math.log2(math.e)),
        grid=(nb, seq // tm),
        in_specs=[tok_spec,
                  pl.BlockSpec((1,) + mod3.shape[1:], lambda b, i: (b, 0, 0)),
                  _resident((d, n_in)),
                  _resident((d, d)),
                  _resident((1, 2 * d)),
                  pl.BlockSpec((tm, LANES), tab),
                  pl.BlockSpec((tm, LANES), tab),
                  pl.BlockSpec((tm, LANES), tab)],
        out_specs=[head_spec, head_spec,
                   pl.BlockSpec((1, n_heads, V_ROWS, tm), lambda b, i: (b, 0, 0, i)),
                   pl.BlockSpec((1, tm, d // LANES, LANES), lambda b, i: (b, i, 0, 0)),
                   pl.BlockSpec((1, tm, 2 * d), tok)],
        out_shape=[head_major, head_major,
                   jax.ShapeDtypeStruct((nb, n_heads, V_ROWS, seq), BF16),
                   jax.ShapeDtypeStruct((nb, seq, d // LANES, LANES), F32),
                   act(2 * d)],
        compiler_params=_compiler_params(2),
        name="in_projection",
    )(x, mod3, w_in, w_v_t, b_gate.reshape(1, 2 * d), *tables)


ATTN_KEY_CHUNK = 512
ATTN_KEY_ROWS = 16


def _attn_kernel(q_ref, k_ref, vt_ref, lam_ref, g_ref, o_ref,
                 s_a, s_b, p_a, p_b, m_a, m_b, *, tq, out_scale):
    heads, seq = k_ref.shape[1], k_ref.shape[2]
    n_blk = seq // tq
    n_items = heads * n_blk
    key_chunk = min(ATTN_KEY_CHUNK, seq)

    def locate(e):
        if isinstance(e, int):
            return e // n_blk, (e % n_blk) * tq
        return lax.div(e, n_blk), pl.multiple_of(lax.rem(e, n_blk) * tq, tq)

    def fold_rows(t, op):
        acc = t[0:SUBLANES]
        for r0 in range(SUBLANES, t.shape[0], SUBLANES):
            acc = op(acc, t[r0:r0 + SUBLANES])
        return acc

    def all_rows(t, reduce):
        return jnp.broadcast_to(reduce(t, axis=0, keepdims=True), t.shape)

    def stage(score_job, softmax_job, value_job):
        if score_job is not None:
            head, row0 = locate(score_job[0])
            q = q_ref[0, head, pl.ds(row0, tq), :]
            lane = lax.broadcasted_iota(jnp.int32, q.shape, 1)
            zero = jnp.zeros_like(q)
            qq = jnp.concatenate([jnp.where(lane < HEAD_DIM, q, zero),
                                  jnp.where(lane >= HEAD_DIM, q, zero)], axis=0)
        if softmax_job is not None:
            m = softmax_job[1][...]
        m_acc = None
        for c0 in range(0, seq, key_chunk):
            keys = slice(c0, c0 + key_chunk)
            if score_job is not None:
                st = lax.dot_general(k_ref[0, head, keys, :], qq, (((1,), (1,)), ((), ())),
                                     preferred_element_type=F32)
                score_job[1][keys, :] = st
                m_c = fold_rows(st, jnp.maximum)
                m_acc = m_c if m_acc is None else jnp.maximum(m_acc, m_c)
            if softmax_job is not None:
                s_buf, _, p_buf = softmax_job
                for r0 in range(c0, c0 + key_chunk, ATTN_KEY_ROWS):
                    parts = []
                    for r in range(r0, r0 + ATTN_KEY_ROWS, SUBLANES):
                        parts.append(jnp.exp2(s_buf[r:r + SUBLANES, :] - m))
                    p_buf[r0:r0 + ATTN_KEY_ROWS, :] = jnp.concatenate(parts, axis=0).astype(BF16)
        if value_job is not None:
            v_head, v_row0 = locate(value_job[0])
            ot = jnp.dot(vt_ref[0, v_head], value_job[1][...], preferred_element_type=F32)
        if score_job is not None:
            score_job[2][...] = all_rows(m_acc, jnp.max)
        if value_job is not None:
            ot = ot[:HEAD_WIDTH] / ot[HEAD_WIDTH:HEAD_WIDTH + 1]
            lam = jnp.concatenate([lam_ref[...]] * (tq // LANES), axis=1)
            att = ot[:, :tq] - lam * ot[:, tq:]
            ms = jnp.mean(att * att, axis=0, keepdims=True)
            y = att * lax.rsqrt(ms + LN_EPS) * g_ref[...] * out_scale
            o_ref[0, v_head, pl.ds(v_row0, tq), :] = y.T.astype(BF16)

    buf_a = (s_a, m_a, p_a)
    buf_b = (s_b, m_b, p_b)
    stage((0, s_a, m_a), None, None)
    stage((1, s_b, m_b), buf_a, None)

    def pair(ee, carry):
        e = 2 * ee
        stage((e + 2, s_a, m_a), buf_b, (e, p_a))
        stage((e + 3, s_b, m_b), buf_a, (e + 1, p_b))
        return carry

    lax.fori_loop(0, n_items // 2 - 1, pair, 0)
    stage(None, buf_b, (n_items - 2, p_a))
    stage(None, None, (n_items - 1, p_b))


def _attention(q, k, v_t, lam, subln_g, lam_init, tq):
    nb, n_heads, seq, _ = q.shape
    heads = _heads_per_step(seq, tq, n_heads)
    assert (heads * (seq // tq)) % 2 == 0 and seq % tq == 0 and tq % LANES == 0
    assert seq % ATTN_KEY_ROWS == 0
    group = lambda b, g: (b, g, 0, 0)
    head_spec = pl.BlockSpec((1, heads, seq, HEAD_WIDTH), group)
    gain = jnp.broadcast_to(subln_g.astype(F32)[:, None], (HEAD_WIDTH, tq))
    score_buf = pltpu.VMEM((seq, 2 * tq), F32)
    prob_buf = pltpu.VMEM((seq, 2 * tq), BF16)
    stat_buf = pltpu.VMEM((SUBLANES, 2 * tq), F32)
    return pl.pallas_call(
        functools.partial(_attn_kernel, tq=tq, out_scale=1.0 - lam_init),
        grid=(nb, n_heads // heads),
        in_specs=[head_spec, head_spec,
                  pl.BlockSpec((1, heads, V_ROWS, seq), group),
                  pl.BlockSpec((1, LANES), lambda b, g: (0, 0)),
                  pl.BlockSpec((HEAD_WIDTH, tq), lambda b, g: (0, 0))],
        out_specs=head_spec,
        out_shape=jax.ShapeDtypeStruct((nb, n_heads, seq, HEAD_WIDTH), BF16),
        scratch_shapes=[score_buf, score_buf, prob_buf, prob_buf,
                        stat_buf, stat_buf],
        compiler_params=_compiler_params(2),
        name="diff_attention",
    )(q, k, v_t, lam, gain)


CONV_HALO = 16
CONV_ROWS = 32


def _mix_kernel(u_ref, up_ref, un_ref, att_ref, g_ref, x_ref, mod_ref,
                cw_ref, cb_ref, clg_ref, clb_ref, wpw_ref, bpw_ref, wo_ref,
                l1g_ref, l1b_ref, o_ref, ubuf, cbuf, *, alpha):
    tm, d = x_ref.shape[1], x_ref.shape[2]
    width = cw_ref.shape[0]
    pad = (width - 1) // 2
    i = pl.program_id(1)
    last = pl.num_programs(1) - 1

    ubuf[0:CONV_HALO] = jnp.where(i > 0, up_ref[0], 0.0)
    ubuf[CONV_HALO:CONV_HALO + tm] = u_ref[0]
    ubuf[CONV_HALO + tm:] = jnp.where(i < last, un_ref[0], 0.0)

    def conv_rows(rb, carry):
        r0 = rb * CONV_ROWS
        acc = jnp.zeros((CONV_ROWS, d // LANES, LANES), F32)
        for j in range(width):
            acc = acc + ubuf[pl.ds(r0 + (CONV_HALO - pad + j), CONV_ROWS)] * cw_ref[j]
        cbuf[pl.ds(r0, CONV_ROWS)] = acc
        return carry

    lax.fori_loop(0, tm // CONV_ROWS, conv_rows, 0)

    conv = cbuf[...].reshape(tm, d) + cb_ref[...]
    y = _norm(conv) * clg_ref[...] + clb_ref[...]
    y = y * jax.nn.sigmoid(y)
    y_b = jnp.dot(y.astype(BF16), wpw_ref[...], preferred_element_type=F32) + bpw_ref[...]
    gates = g_ref[0]
    att = jnp.concatenate([att_ref[0, j] for j in range(att_ref.shape[1])], axis=1)
    merged = gates[:, :d] * att + gates[:, d:] * y_b
    mix = jnp.dot(merged.astype(BF16), wo_ref[...], preferred_element_type=F32)
    g1 = mod_ref[0, 2:3, :]
    z = alpha * x_ref[0] + g1 * mix
    o_ref[0] = _norm(z) * l1g_ref[...] + l1b_ref[...]


def _mixer_output(u, att, gates, x, mod3, conv_w, conv_b, conv_ln_g, conv_ln_b,
                  w_pw2, b_pw2, w_o, ln1_g, ln1_b, alpha, tm):
    nb, seq, d = x.shape
    width = conv_w.shape[0]
    n_heads = att.shape[1]
    assert (width - 1) // 2 < CONV_HALO and tm % CONV_ROWS == 0 and tm % CONV_HALO == 0
    hb = tm // CONV_HALO
    n_halo = seq // CONV_HALO
    rows3 = (d // LANES, LANES)
    tok = lambda b, i: (b, i, 0)
    row = lambda v: v.reshape(1, d).astype(F32)
    vec = pl.BlockSpec((1, d), lambda b, i: (0, 0))
    return pl.pallas_call(
        functools.partial(_mix_kernel, alpha=alpha),
        grid=(nb, seq // tm),
        in_specs=[pl.BlockSpec((1, tm) + rows3, lambda b, i: (b, i, 0, 0)),
                  pl.BlockSpec((1, CONV_HALO) + rows3,
                               lambda b, i: (b, jnp.maximum(i * hb - 1, 0), 0, 0)),
                  pl.BlockSpec((1, CONV_HALO) + rows3,
                               lambda b, i: (b, jnp.minimum((i + 1) * hb, n_halo - 1), 0, 0)),
                  pl.BlockSpec((1, n_heads, tm, HEAD_WIDTH), lambda b, i: (b, 0, i, 0)),
                  pl.BlockSpec((1, tm, 2 * d), tok),
                  pl.BlockSpec((1, tm, d), tok),
                  pl.BlockSpec((1,) + mod3.shape[1:], lambda b, i: (b, 0, 0)),
                  _resident((width, d // LANES, LANES)), vec, vec, vec,
                  _resident((d, d)), vec,
                  _resident((d, d)), vec, vec],
        out_specs=pl.BlockSpec((1, tm, d), tok),
        out_shape=jax.ShapeDtypeStruct((nb, seq, d), F32),
        scratch_shapes=[pltpu.VMEM((tm + 2 * CONV_HALO, d // LANES, LANES), F32),
                        pltpu.VMEM((tm, d // LANES, LANES), F32)],
        compiler_params=_compiler_params(2),
        name="mixer_output",
    )(u, u, u, att, gates, x, mod3, conv_w.astype(F32).reshape(width, d // LANES, LANES),
      row(conv_b), row(conv_ln_g),
      row(conv_ln_b), w_pw2, row(b_pw2), w_o, row(ln1_g), row(ln1_b))


FFN_HALO = 8
FFN_CHUNK = 256
GELU_FOLD = 2.0 ** -0.5


def _ffn_kernel(x_ref, xp_ref, xn_ref, mod_ref, wup_ref, fcw_ref, fcb_ref, wdn_ref,
                l2g_ref, l2b_ref, o_ref, gbuf, vbuf, abuf, *, alpha, d_ff):
    tm = x_ref.shape[1]
    i = pl.program_id(1)
    last = pl.num_programs(1) - 1
    sh2 = mod_ref[0, 3:4, :]
    sc2 = mod_ref[0, 4:5, :]
    g2 = mod_ref[0, 5:6, :]

    def modnorm(t):
        return _norm(t) * (1.0 + sc2) + sh2

    x = x_ref[0]
    h_prev = jnp.where(i > 0, modnorm(xp_ref[0]), 0.0)
    h_next = jnp.where(i < last, modnorm(xn_ref[0]), 0.0)
    h_ext = jnp.concatenate([h_prev, modnorm(x), h_next], axis=0).astype(BF16)

    def conv3(buf, c0):
        cols = slice(c0, c0 + FFN_CHUNK)
        up = buf[...]
        rows = up.shape[0]
        main = slice(FFN_HALO, FFN_HALO + tm)
        return (pltpu.roll(up, 1, 0)[main] * fcw_ref[0:1, cols]
                + up[main] * fcw_ref[1:2, cols]
                + pltpu.roll(up, rows - 1, 0)[main] * fcw_ref[2:3, cols]
                + fcb_ref[:, cols])

    for c in range(d_ff // FFN_CHUNK):
        c0 = c * FFN_CHUNK
        gbuf[...] = jnp.dot(h_ext, wup_ref[:, c0:c0 + FFN_CHUNK], preferred_element_type=F32)
        vbuf[...] = jnp.dot(h_ext, wup_ref[:, d_ff + c0:d_ff + c0 + FFN_CHUNK],
                            preferred_element_type=F32)
        fg = conv3(gbuf, c0)
        fv = conv3(vbuf, d_ff + c0)
        abuf[:, c0:c0 + FFN_CHUNK] = (fg * fv * (1.0 + lax.erf(fg))).astype(BF16)

    f = jnp.dot(abuf[...], wdn_ref[...], preferred_element_type=F32)
    z = alpha * x + g2 * f
    o_ref[0] = _norm(z) * l2g_ref[...] + l2b_ref[...]


def _channel_mixer(x1, mod3, w_up, ffn_conv_w, ffn_conv_b, w_down, ln2_g, ln2_b, alpha, tm):
    nb, seq, d = x1.shape
    d_ff = w_down.shape[0]
    assert d_ff % FFN_CHUNK == 0 and ffn_conv_w.shape[0] == 3 and tm % FFN_HALO == 0
    hb = tm // FFN_HALO
    n_halo = seq // FFN_HALO
    tok = lambda b, i: (b, i, 0)
    vec = pl.BlockSpec((1, d), lambda b, i: (0, 0))
    return pl.pallas_call(
        functools.partial(_ffn_kernel, alpha=alpha, d_ff=d_ff),
        grid=(nb, seq // tm),
        in_specs=[pl.BlockSpec((1, tm, d), tok),
                  pl.BlockSpec((1, FFN_HALO, d), lambda b, i: (b, jnp.maximum(i * hb - 1, 0), 0)),
                  pl.BlockSpec((1, FFN_HALO, d),
                               lambda b, i: (b, jnp.minimum((i + 1) * hb, n_halo - 1), 0)),
                  pl.BlockSpec((1,) + mod3.shape[1:], lambda b, i: (b, 0, 0)),
                  _resident((d, 2 * d_ff)),
                  _resident((3, 2 * d_ff)),
                  _resident((1, 2 * d_ff)),
                  _resident((d_ff, d)), vec, vec],
        out_specs=pl.BlockSpec((1, tm, d), tok),
        out_shape=jax.ShapeDtypeStruct((nb, seq, d), F32),
        scratch_shapes=[pltpu.VMEM((tm + 2 * FFN_HALO, FFN_CHUNK), F32),
                        pltpu.VMEM((tm + 2 * FFN_HALO, FFN_CHUNK), F32),
                        pltpu.VMEM((tm, d_ff), BF16)],
        compiler_params=_compiler_params(2),
        name="channel_mixer",
    )(x1, x1, x1, mod3, w_up, ffn_conv_w.astype(F32) * GELU_FOLD,
      ffn_conv_b.reshape(1, 2 * d_ff).astype(F32) * GELU_FOLD,
      w_down, ln2_g.reshape(1, d).astype(F32), ln2_b.reshape(1, d).astype(F32))


def _encoder_layer(x, mod3, lam, lam_init, alpha, w_in, w_v_t, b_gate, subln_g, conv_w, conv_b,
                   conv_ln_g, conv_ln_b, w_pw2, b_pw2, w_o, ln1_g, ln1_b, w_up, ffn_conv_w,
                   ffn_conv_b, w_down, ln2_g, ln2_b):
    seq = x.shape[1]
    tm, tq = _tiles(seq)
    q, k, v_t, u, gates = _in_projection(x, mod3, w_in, w_v_t, b_gate, _rope_tables(seq), tm)
    att = _attention(q, k, v_t, lam, subln_g, lam_init, tq)
    x1 = _mixer_output(u, att, gates, x, mod3, conv_w, conv_b, conv_ln_g, conv_ln_b,
                       w_pw2, b_pw2, w_o, ln1_g, ln1_b, alpha, tm)
    return _channel_mixer(x1, mod3, w_up, ffn_conv_w, ffn_conv_b, w_down, ln2_g, ln2_b, alpha, tm)


def kernel(x_prompt, x_sample, c_prompt, c_sample, w_ada, b_ada, w_in, b_gate, lambda_q1, lambda_k1, lambda_q2, lambda_k2, subln_g, conv_w, conv_b, conv_ln_g, conv_ln_b, w_pw2, b_pw2, w_o, ln1_g, ln1_b, w_up, ffn_conv_w, ffn_conv_b, w_down, ln2_g, ln2_b):
    depth = w_ada.shape[0]
    d = x_prompt.shape[-1]
    alpha = (2.0 * depth) ** 0.25
    n_prompt = c_prompt.shape[0]
    y_prompt, y_sample = x_prompt, x_sample
    for l in range(depth):
        lam_init = 0.8 - 0.6 * math.exp(-0.3 * l)
        mod = _modulation(jnp.concatenate([c_prompt, c_sample], axis=0), w_ada[l], b_ada[l])
        mod3 = mod.reshape(mod.shape[0], -1, d)
        lam = _lambda(lambda_q1[l], lambda_k1[l], lambda_q2[l], lambda_k2[l], lam_init)
        w_v_t = w_in[l][:, 2 * d:3 * d].T.astype(BF16)
        weights = (w_in[l].astype(BF16), w_v_t, b_gate[l], subln_g[l], conv_w[l], conv_b[l],
                   conv_ln_g[l], conv_ln_b[l], w_pw2[l].astype(BF16), b_pw2[l],
                   w_o[l].astype(BF16), ln1_g[l], ln1_b[l], w_up[l].astype(BF16),
                   ffn_conv_w[l], ffn_conv_b[l], w_down[l].astype(BF16), ln2_g[l], ln2_b[l])
        y_prompt = _encoder_layer(y_prompt, mod3[:n_prompt], lam, lam_init, alpha, *weights)
        y_sample = _encoder_layer(y_sample, mod3[n_prompt:], lam, lam_init, alpha, *weights)
    return (y_prompt, y_sample)
```

```python
import functools
import math

import jax
import jax.numpy as jnp
from jax import lax
from jax.experimental import pallas as pl
from jax.experimental.pallas import tpu as pltpu

F32 = jnp.float32
BF16 = jnp.bfloat16

HEAD_DIM = 64
HEAD_WIDTH = 2 * HEAD_DIM
ROT_DIM = HEAD_DIM // 4
ROPE_THETA = 500000.0
LN_EPS = 1e-5
LANES = 128
SUBLANES = 8
SUM_ROWS = 16
V_ROWS = HEAD_WIDTH + SUM_ROWS
VMEM_LIMIT_BYTES = 56 * 1024 * 1024
ATTN_VMEM_BUDGET = 48 * 1024 * 1024
SCORE_BUFFER_BYTES = 8 * 1024 * 1024


def _compiler_params(n_grid_dims):
    return pltpu.CompilerParams(
        dimension_semantics=("arbitrary",) * n_grid_dims,
        vmem_limit_bytes=VMEM_LIMIT_BYTES)


def _resident(shape):
    zeros = (0,) * len(shape)
    return pl.BlockSpec(shape, lambda *_: zeros, pipeline_mode=pl.Buffered(1))


def _tiles(seq):
    tm = min(512, seq)
    tq = LANES
    while 2 * tq <= seq // 2 and seq * 4 * tq * 4 <= SCORE_BUFFER_BYTES:
        tq *= 2
    return tm, tq


def _heads_per_step(seq, tq, n_heads):
    scratch = 2 * seq * 2 * tq * (4 + 2)
    per_head = 2 * seq * (3 * HEAD_WIDTH + V_ROWS) * 2
    heads = n_heads
    while heads > 1 and scratch + heads * per_head > ATTN_VMEM_BUDGET:
        heads //= 2
    return heads


def _norm(x):
    mu = jnp.mean(x, axis=-1, keepdims=True)
    xc = x - mu
    var = jnp.mean(xc * xc, axis=-1, keepdims=True)
    return xc * lax.rsqrt(var + LN_EPS)


def _mod_kernel(c_ref, w_ref, b_ref, o_ref):
    c = c_ref[...]
    a = c * jax.nn.sigmoid(c)
    o_ref[...] = jnp.dot(a, w_ref[...], preferred_element_type=F32,
                         precision=lax.Precision.HIGHEST) + b_ref[...]


def _modulation(c, w_ada, b_ada):
    nb, d = c.shape
    n_out = w_ada.shape[1]
    return pl.pallas_call(
        _mod_kernel,
        grid=(n_out // d,),
        in_specs=[pl.BlockSpec((nb, d), lambda j: (0, 0)),
                  pl.BlockSpec((d, d), lambda j: (0, j)),
                  pl.BlockSpec((1, d), lambda j: (0, j))],
        out_specs=pl.BlockSpec((nb, d), lambda j: (0, j)),
        out_shape=jax.ShapeDtypeStruct((nb, n_out), F32),
        compiler_params=_compiler_params(1),
        name="adaln_modulation",
    )(c, w_ada, b_ada.reshape(1, n_out))


def _lambda_kernel(l_ref, o_ref, *, lam_init):
    a = jnp.sum(l_ref[0:1, :] * l_ref[1:2, :], axis=-1, keepdims=True)
    b = jnp.sum(l_ref[2:3, :] * l_ref[3:4, :], axis=-1, keepdims=True)
    lam = jnp.exp(a) - jnp.exp(b) + lam_init
    o_ref[...] = jnp.broadcast_to(lam, o_ref.shape)


def _lambda(lq1, lk1, lq2, lk2, lam_init):
    stacked = jnp.stack([lq1, lk1, lq2, lk2]).astype(F32)
    return pl.pallas_call(
        functools.partial(_lambda_kernel, lam_init=lam_init),
        out_shape=jax.ShapeDtypeStruct((1, LANES), F32),
        name="diff_lambda",
    )(stacked)


def _rope_tables(seq):
    half = ROT_DIM // 2
    inv_freq = ROPE_THETA ** (-jnp.arange(0, ROT_DIM, 2, dtype=F32) / ROT_DIM)
    ang = jnp.arange(seq, dtype=F32)[:, None] * inv_freq[None, :]
    cos, sin = jnp.cos(ang), jnp.sin(ang)
    ones = jnp.ones((seq, HEAD_DIM - ROT_DIM), F32)
    zeros = jnp.zeros((seq, HEAD_DIM - ROT_DIM), F32)
    zh = jnp.zeros((seq, half), F32)
    cos_t = jnp.concatenate([cos, cos, ones], axis=1)
    sin_a = jnp.concatenate([-sin, zh, zeros], axis=1)
    sin_b = jnp.concatenate([zh, sin, zeros], axis=1)
    rep = LANES // HEAD_DIM
    return tuple(jnp.tile(t, (1, rep)) for t in (cos_t, sin_a, sin_b))


def _inproj_kernel(x_ref, mod_ref, w_ref, wvt_ref, bg_ref, cos_ref, sa_ref, sb_ref,
                   q_ref, k_ref, vt_ref, u_ref, g_ref, *, d, qk_scale):
    half = ROT_DIM // 2
    sh1 = mod_ref[0, 0:1, :]
    sc1 = mod_ref[0, 1:2, :]
    h = (_norm(x_ref[0]) * (1.0 + sc1) + sh1).astype(BF16)

    def proj(c0, width):
        return jnp.dot(h, w_ref[:, c0:c0 + width], preferred_element_type=F32)

    cos_t, sin_a, sin_b = cos_ref[...], sa_ref[...], sb_ref[...]

    def store_rope(t, o_ref, scale):
        for j in range(d // HEAD_WIDTH):
            s = t[:, j * HEAD_WIDTH:(j + 1) * HEAD_WIDTH]
            r = (s * cos_t + pltpu.roll(s, LANES - half, 1) * sin_a
                 + pltpu.roll(s, half, 1) * sin_b)
            if scale != 1.0:
                r = r * scale
            o_ref[0, j] = r.astype(BF16)

    store_rope(proj(0, d), q_ref, qk_scale)
    store_rope(proj(d, d), k_ref, 1.0)
    v_t = lax.dot_general(wvt_ref[...], h, (((1,), (1,)), ((), ())),
                          preferred_element_type=F32).astype(BF16)
    n_heads = d // HEAD_WIDTH
    vt_ref[0, :, 0:HEAD_WIDTH, :] = v_t.reshape(n_heads, HEAD_WIDTH, v_t.shape[1])
    vt_ref[0, :, HEAD_WIDTH:, :] = jnp.ones((n_heads, SUM_ROWS, v_t.shape[1]), BF16)
    glu_a = proj(3 * d, d)
    glu_b = proj(4 * d, d)
    u_ref[0] = (glu_a * jax.nn.sigmoid(glu_b)).reshape(u_ref.shape[1:])
    gate = proj(5 * d, 2 * d) + bg_ref[...]
    g_ref[0] = jax.nn.sigmoid(gate).astype(BF16)


def _in_projection(x, mod3, w_in, w_v_t, b_gate, tables, tm):
    nb, seq, d = x.shape
    n_in = w_in.shape[1]
    n_heads = d // HEAD_WIDTH
    assert HEAD_WIDTH == LANES
    tok = lambda b, i: (b, i, 0)
    tab = lambda b, i: (i, 0)
    act = lambda width: jax.ShapeDtypeStruct((nb, seq, width), BF16)
    tok_spec = pl.BlockSpec((1, tm, d), tok)
    head_spec = pl.BlockSpec((1, n_heads, tm, HEAD_WIDTH), lambda b, i: (b, 0, i, 0))
    head_major = jax.ShapeDtypeStruct((nb, n_heads, seq, HEAD_WIDTH), BF16)
    return pl.pallas_call(
        functools.partial(_inproj_kernel, d=d, qk_scale=HEAD_DIM ** -0.5 * math.log2(math.e)),
        grid=(nb, seq // tm),
        in_specs=[tok_spec,
                  pl.BlockSpec((1,) + mod3.shape[1:], lambda b, i: (b, 0, 0)),
                  _resident((d, n_in)),
                  _resident((d, d)),
                  _resident((1, 2 * d)),
                  pl.BlockSpec((tm, LANES), tab),
                  pl.BlockSpec((tm, LANES), tab),
                  pl.BlockSpec((tm, LANES), tab)],
        out_specs=[head_spec, head_spec,
                   pl.BlockSpec((1, n_heads, V_ROWS, tm), lambda b, i: (b, 0, 0, i)),
                   pl.BlockSpec((1, tm, d // LANES, LANES), lambda b, i: (b, i, 0, 0)),
                   pl.BlockSpec((1, tm, 2 * d), tok)],
        out_shape=[head_major, head_major,
                   jax.ShapeDtypeStruct((nb, n_heads, V_ROWS, seq), BF16),
                   jax.ShapeDtypeStruct((nb, seq, d // LANES, LANES), F32),
                   act(2 * d)],
        compiler_params=_compiler_params(2),
        name="in_projection",
    )(x, mod3, w_in, w_v_t, b_gate.reshape(1, 2 * d), *tables)


ATTN_KEY_CHUNK = 512
ATTN_KEY_ROWS = 16


def _attn_kernel(q_ref, k_ref, vt_ref, lam_ref, g_ref, o_ref,
                 s_a, s_b, p_a, p_b, m_a, m_b, *, tq, out_scale):
    heads, seq = k_ref.shape[1], k_ref.shape[2]
    n_blk = seq // tq
    n_items = heads * n_blk
    key_chunk = min(ATTN_KEY_CHUNK, seq)

    def locate(e):
        if isinstance(e, int):
            return e // n_blk, (e % n_blk) * tq
        return lax.div(e, n_blk), pl.multiple_of(lax.rem(e, n_blk) * tq, tq)

    def fold_rows(t, op):
        acc = t[0:SUBLANES]
        for r0 in range(SUBLANES, t.shape[0], SUBLANES):
            acc = op(acc, t[r0:r0 + SUBLANES])
        return acc

    def all_rows(t, reduce):
        return jnp.broadcast_to(reduce(t, axis=0, keepdims=True), t.shape)

    def stage(score_job, softmax_job, value_job):
        if score_job is not None:
            head, row0 = locate(score_job[0])
            q = q_ref[0, head, pl.ds(row0, tq), :]
            lane = lax.broadcasted_iota(jnp.int32, q.shape, 1)
            zero = jnp.zeros_like(q)
            qq = jnp.concatenate([jnp.where(lane < HEAD_DIM, q, zero),
                                  jnp.where(lane >= HEAD_DIM, q, zero)], axis=0)
        if softmax_job is not None:
            m = softmax_job[1][...]
        m_acc = None
        for c0 in range(0, seq, key_chunk):
            keys = slice(c0, c0 + key_chunk)
            if score_job is not None:
                st = lax.dot_general(k_ref[0, head, keys, :], qq, (((1,), (1,)), ((), ())),
                                     preferred_element_type=F32)
                score_job[1][keys, :] = st
                m_c = fold_rows(st, jnp.maximum)
                m_acc = m_c if m_acc is None else jnp.maximum(m_acc, m_c)
            if softmax_job is not None:
                s_buf, _, p_buf = softmax_job
                for r0 in range(c0, c0 + key_chunk, ATTN_KEY_ROWS):
                    parts = []
                    for r in range(r0, r0 + ATTN_KEY_ROWS, SUBLANES):
                        parts.append(jnp.exp2(s_buf[r:r + SUBLANES, :] - m))
                    p_buf[r0:r0 + ATTN_KEY_ROWS, :] = jnp.concatenate(parts, axis=0).astype(BF16)
        if value_job is not None:
            v_head, v_row0 = locate(value_job[0])
            ot = jnp.dot(vt_ref[0, v_head], value_job[1][...], preferred_element_type=F32)
        if score_job is not None:
            score_job[2][...] = all_rows(m_acc, jnp.max)
        if value_job is not None:
            ot = ot[:HEAD_WIDTH] / ot[HEAD_WIDTH:HEAD_WIDTH + 1]
            lam = jnp.concatenate([lam_ref[...]] * (tq // LANES), axis=1)
            att = ot[:, :tq] - lam * ot[:, tq:]
            ms = jnp.mean(att * att, axis=0, keepdims=True)
            y = att * lax.rsqrt(ms + LN_EPS) * g_ref[...] * out_scale
            o_ref[0, v_head, pl.ds(v_row0, tq), :] = y.T.astype(BF16)

    buf_a = (s_a, m_a, p_a)
    buf_b = (s_b, m_b, p_b)
    stage((0, s_a, m_a), None, None)
    stage((1, s_b, m_b), buf_a, None)

    def pair(ee, carry):
        e = 2 * ee
        stage((e + 2, s_a, m_a), buf_b, (e, p_a))
        stage((e + 3, s_b, m_b), buf_a, (e + 1, p_b))
        return carry

    lax.fori_loop(0, n_items // 2 - 1, pair, 0)
    stage(None, buf_b, (n_items - 2, p_a))
    stage(None, None, (n_items - 1, p_b))


def _attention(q, k, v_t, lam, subln_g, lam_init, tq):
    nb, n_heads, seq, _ = q.shape
    heads = _heads_per_step(seq, tq, n_heads)
    assert (heads * (seq // tq)) % 2 == 0 and seq % tq == 0 and tq % LANES == 0
    assert seq % ATTN_KEY_ROWS == 0
    group = lambda b, g: (b, g, 0, 0)
    head_spec = pl.BlockSpec((1, heads, seq, HEAD_WIDTH), group)
    gain = jnp.broadcast_to(subln_g.astype(F32)[:, None], (HEAD_WIDTH, tq))
    score_buf = pltpu.VMEM((seq, 2 * tq), F32)
    prob_buf = pltpu.VMEM((seq, 2 * tq), BF16)
    stat_buf = pltpu.VMEM((SUBLANES, 2 * tq), F32)
    return pl.pallas_call(
        functools.partial(_attn_kernel, tq=tq, out_scale=1.0 - lam_init),
        grid=(nb, n_heads // heads),
        in_specs=[head_spec, head_spec,
                  pl.BlockSpec((1, heads, V_ROWS, seq), group),
                  pl.BlockSpec((1, LANES), lambda b, g: (0, 0)),
                  pl.BlockSpec((HEAD_WIDTH, tq), lambda b, g: (0, 0))],
        out_specs=head_spec,
        out_shape=jax.ShapeDtypeStruct((nb, n_heads, seq, HEAD_WIDTH), BF16),
        scratch_shapes=[score_buf, score_buf, prob_buf, prob_buf,
                        stat_buf, stat_buf],
        compiler_params=_compiler_params(2),
        name="diff_attention",
    )(q, k, v_t, lam, gain)


CONV_HALO = 16
CONV_ROWS = 32


def _mix_kernel(u_ref, up_ref, un_ref, att_ref, g_ref, x_ref, mod_ref,
                cw_ref, cb_ref, clg_ref, clb_ref, wpw_ref, bpw_ref, wo_ref,
                l1g_ref, l1b_ref, o_ref, ubuf, cbuf, *, alpha):
    tm, d = x_ref.shape[1], x_ref.shape[2]
    width = cw_ref.shape[0]
    pad = (width - 1) // 2
    i = pl.program_id(1)
    last = pl.num_programs(1) - 1

    ubuf[0:CONV_HALO] = jnp.where(i > 0, up_ref[0], 0.0)
    ubuf[CONV_HALO:CONV_HALO + tm] = u_ref[0]
    ubuf[CONV_HALO + tm:] = jnp.where(i < last, un_ref[0], 0.0)

    def conv_rows(rb, carry):
        r0 = rb * CONV_ROWS
        acc = jnp.zeros((CONV_ROWS, d // LANES, LANES), F32)
        for j in range(width):
            acc = acc + ubuf[pl.ds(r0 + (CONV_HALO - pad + j), CONV_ROWS)] * cw_ref[j]
        cbuf[pl.ds(r0, CONV_ROWS)] = acc
        return carry

    lax.fori_loop(0, tm // CONV_ROWS, conv_rows, 0, unroll=2)

    conv = cbuf[...].reshape(tm, d) + cb_ref[...]
    y = _norm(conv) * clg_ref[...] + clb_ref[...]
    y = y * jax.nn.sigmoid(y)
    y_b = jnp.dot(y.astype(BF16), wpw_ref[...], preferred_element_type=F32) + bpw_ref[...]
    gates = g_ref[0]
    att = jnp.concatenate([att_ref[0, j] for j in range(att_ref.shape[1])], axis=1)
    merged = gates[:, :d] * att + gates[:, d:] * y_b
    mix = jnp.dot(merged.astype(BF16), wo_ref[...], preferred_element_type=F32)
    g1 = mod_ref[0, 2:3, :]
    z = alpha * x_ref[0] + g1 * mix
    o_ref[0] = _norm(z) * l1g_ref[...] + l1b_ref[...]


def _mixer_output(u, att, gates, x, mod3, conv_w, conv_b, conv_ln_g, conv_ln_b,
                  w_pw2, b_pw2, w_o, ln1_g, ln1_b, alpha, tm):
    nb, seq, d = x.shape
    width = conv_w.shape[0]
    n_heads = att.shape[1]
    assert (width - 1) // 2 < CONV_HALO and tm % CONV_ROWS == 0 and tm % CONV_HALO == 0
    hb = tm // CONV_HALO
    n_halo = seq // CONV_HALO
    rows3 = (d // LANES, LANES)
    tok = lambda b, i: (b, i, 0)
    row = lambda v: v.reshape(1, d).astype(F32)
    vec = pl.BlockSpec((1, d), lambda b, i: (0, 0))
    return pl.pallas_call(
        functools.partial(_mix_kernel, alpha=alpha),
        grid=(nb, seq // tm),
        in_specs=[pl.BlockSpec((1, tm) + rows3, lambda b, i: (b, i, 0, 0)),
                  pl.BlockSpec((1, CONV_HALO) + rows3,
                               lambda b, i: (b, jnp.maximum(i * hb - 1, 0), 0, 0)),
                  pl.BlockSpec((1, CONV_HALO) + rows3,
                               lambda b, i: (b, jnp.minimum((i + 1) * hb, n_halo - 1), 0, 0)),
                  pl.BlockSpec((1, n_heads, tm, HEAD_WIDTH), lambda b, i: (b, 0, i, 0)),
                  pl.BlockSpec((1, tm, 2 * d), tok),
                  pl.BlockSpec((1, tm, d), tok),
                  pl.BlockSpec((1,) + mod3.shape[1:], lambda b, i: (b, 0, 0)),
                  _resident((width, d // LANES, LANES)), vec, vec, vec,
                  _resident((d, d)), vec,
                  _resident((d, d)), vec, vec],
        out_specs=pl.BlockSpec((1, tm, d), tok),
        out_shape=jax.ShapeDtypeStruct((nb, seq, d), F32),
        scratch_shapes=[pltpu.VMEM((tm + 2 * CONV_HALO, d // LANES, LANES), F32),
                        pltpu.VMEM((tm, d // LANES, LANES), F32)],
        compiler_params=_compiler_params(2),
        name="mixer_output",
    )(u, u, u, att, gates, x, mod3, conv_w.astype(F32).reshape(width, d // LANES, LANES),
      row(conv_b), row(conv_ln_g),
      row(conv_ln_b), w_pw2, row(b_pw2), w_o, row(ln1_g), row(ln1_b))


FFN_HALO = 8
FFN_CHUNK = 256
GELU_FOLD = 2.0 ** -0.5


def _ffn_kernel(x_ref, xp_ref, xn_ref, mod_ref, wup_ref, fcw_ref, fcb_ref, wdn_ref,
                l2g_ref, l2b_ref, o_ref, gbuf, vbuf, abuf, *, alpha, d_ff):
    tm = x_ref.shape[1]
    i = pl.program_id(1)
    last = pl.num_programs(1) - 1
    sh2 = mod_ref[0, 3:4, :]
    sc2 = mod_ref[0, 4:5, :]
    g2 = mod_ref[0, 5:6, :]

    def modnorm(t):
        return _norm(t) * (1.0 + sc2) + sh2

    x = x_ref[0]
    h_prev = jnp.where(i > 0, modnorm(xp_ref[0]), 0.0)
    h_next = jnp.where(i < last, modnorm(xn_ref[0]), 0.0)
    h_ext = jnp.concatenate([h_prev, modnorm(x), h_next], axis=0).astype(BF16)

    def conv3(buf, c0):
        cols = slice(c0, c0 + FFN_CHUNK)
        up = buf[...]
        rows = up.shape[0]
        main = slice(FFN_HALO, FFN_HALO + tm)
        return (pltpu.roll(up, 1, 0)[main] * fcw_ref[0:1, cols]
                + up[main] * fcw_ref[1:2, cols]
                + pltpu.roll(up, rows - 1, 0)[main] * fcw_ref[2:3, cols]
                + fcb_ref[:, cols])

    for c in range(d_ff // FFN_CHUNK):
        c0 = c * FFN_CHUNK
        gbuf[...] = jnp.dot(h_ext, wup_ref[:, c0:c0 + FFN_CHUNK], preferred_element_type=F32)
        vbuf[...] = jnp.dot(h_ext, wup_ref[:, d_ff + c0:d_ff + c0 + FFN_CHUNK],
                            preferred_element_type=F32)
        fg = conv3(gbuf, c0)
        fv = conv3(vbuf, d_ff + c0)
        abuf[:, c0:c0 + FFN_CHUNK] = (fg * fv * (1.0 + lax.erf(fg))).astype(BF16)

    f = jnp.dot(abuf[...], wdn_ref[...], preferred_element_type=F32)
    z = alpha * x + g2 * f
    o_ref[0] = _norm(z) * l2g_ref[...] + l2b_ref[...]


def _channel_mixer(x1, mod3, w_up, ffn_conv_w, ffn_conv_b, w_down, ln2_g, ln2_b, alpha, tm):
    nb, seq, d = x1.shape
    d_ff = w_down.shape[0]
    assert d_ff % FFN_CHUNK == 0 and ffn_conv_w.shape[0] == 3 and tm % FFN_HALO == 0
    hb = tm // FFN_HALO
    n_halo = seq // FFN_HALO
    tok = lambda b, i: (b, i, 0)
    vec = pl.BlockSpec((1, d), lambda b, i: (0, 0))
    return pl.pallas_call(
        functools.partial(_ffn_kernel, alpha=alpha, d_ff=d_ff),
        grid=(nb, seq // tm),
        in_specs=[pl.BlockSpec((1, tm, d), tok),
                  pl.BlockSpec((1, FFN_HALO, d), lambda b, i: (b, jnp.maximum(i * hb - 1, 0), 0)),
                  pl.BlockSpec((1, FFN_HALO, d),
                               lambda b, i: (b, jnp.minimum((i + 1) * hb, n_halo - 1), 0)),
                  pl.BlockSpec((1,) + mod3.shape[1:], lambda b, i: (b, 0, 0)),
                  _resident((d, 2 * d_ff)),
                  _resident((3, 2 * d_ff)),
                  _resident((1, 2 * d_ff)),
                  _resident((d_ff, d)), vec, vec],
        out_specs=pl.BlockSpec((1, tm, d), tok),
        out_shape=jax.ShapeDtypeStruct((nb, seq, d), F32),
        scratch_shapes=[pltpu.VMEM((tm + 2 * FFN_HALO, FFN_CHUNK), F32),
                        pltpu.VMEM((tm + 2 * FFN_HALO, FFN_CHUNK), F32),
                        pltpu.VMEM((tm, d_ff), BF16)],
        compiler_params=_compiler_params(2),
        name="channel_mixer",
    )(x1, x1, x1, mod3, w_up, ffn_conv_w.astype(F32) * GELU_FOLD,
      ffn_conv_b.reshape(1, 2 * d_ff).astype(F32) * GELU_FOLD,
      w_down, ln2_g.reshape(1, d).astype(F32), ln2_b.reshape(1, d).astype(F32))


def _encoder_layer(x, mod3, lam, lam_init, alpha, w_in, w_v_t, b_gate, subln_g, conv_w, conv_b,
                   conv_ln_g, conv_ln_b, w_pw2, b_pw2, w_o, ln1_g, ln1_b, w_up, ffn_conv_w,
                   ffn_conv_b, w_down, ln2_g, ln2_b):
    seq = x.shape[1]
    tm, tq = _tiles(seq)
    q, k, v_t, u, gates = _in_projection(x, mod3, w_in, w_v_t, b_gate, _rope_tables(seq), tm)
    att = _attention(q, k, v_t, lam, subln_g, lam_init, tq)
    x1 = _mixer_output(u, att, gates, x, mod3, conv_w, conv_b, conv_ln_g, conv_ln_b,
                       w_pw2, b_pw2, w_o, ln1_g, ln1_b, alpha, tm)
    return _channel_mixer(x1, mod3, w_up, ffn_conv_w, ffn_conv_b, w_down, ln2_g, ln2_b, alpha, tm)


def kernel(x_prompt, x_sample, c_prompt, c_sample, w_ada, b_ada, w_in, b_gate, lambda_q1, lambda_k1, lambda_q2, lambda_k2, subln_g, conv_w, conv_b, conv_ln_g, conv_ln_b, w_pw2, b_pw2, w_o, ln1_g, ln1_b, w_up, ffn_conv_w, ffn_conv_b, w_down, ln2_g, ln2_b):
    depth = w_ada.shape[0]
    d = x_prompt.shape[-1]
    alpha = (2.0 * depth) ** 0.25
    n_prompt = c_prompt.shape[0]
    y_prompt, y_sample = x_prompt, x_sample
    for l in range(depth):
        lam_init = 0.8 - 0.6 * math.exp(-0.3 * l)
        mod = _modulation(jnp.concatenate([c_prompt, c_sample], axis=0), w_ada[l], b_ada[l])
        mod3 = mod.reshape(mod.shape[0], -1, d)
        lam = _lambda(lambda_q1[l], lambda_k1[l], lambda_q2[l], lambda_k2[l], lam_init)
        w_v_t = w_in[l][:, 2 * d:3 * d].T.astype(BF16)
        weights = (w_in[l].astype(BF16), w_v_t, b_gate[l], subln_g[l], conv_w[l], conv_b[l],
                   conv_ln_g[l], conv_ln_b[l], w_pw2[l].astype(BF16), b_pw2[l],
                   w_o[l].astype(BF16), ln1_g[l], ln1_b[l], w_up[l].astype(BF16),
                   ffn_conv_w[l], ffn_conv_b[l], w_down[l].astype(BF16), ln2_g[l], ln2_b[l])
        y_prompt = _encoder_layer(y_prompt, mod3[:n_prompt], lam, lam_init, alpha, *weights)
        y_sample = _encoder_layer(y_sample, mod3[n_prompt:], lam, lam_init, alpha, *weights)
    return (y_prompt, y_sample)
```

```python
import functools
import math

import jax
import jax.numpy as jnp
from jax import lax
from jax.experimental import pallas as pl
from jax.experimental.pallas import tpu as pltpu

F32 = jnp.float32
BF16 = jnp.bfloat16

HEAD_DIM = 64
HEAD_WIDTH = 2 * HEAD_DIM
ROT_DIM = HEAD_DIM // 4
ROPE_THETA = 500000.0
LN_EPS = 1e-5
LANES = 128
SUBLANES = 8
SUM_ROWS = 16
V_ROWS = HEAD_WIDTH + SUM_ROWS
VMEM_LIMIT_BYTES = 56 * 1024 * 1024
ATTN_VMEM_BUDGET = 48 * 1024 * 1024
SCORE_BUFFER_BYTES = 8 * 1024 * 1024


def _compiler_params(n_grid_dims):
    return pltpu.CompilerParams(
        dimension_semantics=("arbitrary",) * n_grid_dims,
        vmem_limit_bytes=VMEM_LIMIT_BYTES)


def _resident(shape):
    zeros = (0,) * len(shape)
    return pl.BlockSpec(shape, lambda *_: zeros, pipeline_mode=pl.Buffered(1))


def _tiles(seq):
    tm = min(512, seq)
    tq = LANES
    while 2 * tq <= seq // 2 and seq * 4 * tq * 4 <= SCORE_BUFFER_BYTES:
        tq *= 2
    return tm, tq


def _heads_per_step(seq, tq, n_heads):
    scratch = 2 * seq * 2 * tq * (4 + 2)
    per_head = 2 * seq * (3 * HEAD_WIDTH + V_ROWS) * 2
    heads = n_heads
    while heads > 1 and scratch + heads * per_head > ATTN_VMEM_BUDGET:
        heads //= 2
    return heads


def _norm(x):
    mu = jnp.mean(x, axis=-1, keepdims=True)
    xc = x - mu
    var = jnp.mean(xc * xc, axis=-1, keepdims=True)
    return xc * lax.rsqrt(var + LN_EPS)


def _mod_kernel(c_ref, w_ref, b_ref, o_ref):
    c = c_ref[...]
    a = c * jax.nn.sigmoid(c)
    o_ref[...] = jnp.dot(a, w_ref[...], preferred_element_type=F32,
                         precision=lax.Precision.HIGHEST) + b_ref[...]


def _modulation(c, w_ada, b_ada):
    nb, d = c.shape
    n_out = w_ada.shape[1]
    return pl.pallas_call(
        _mod_kernel,
        grid=(n_out // d,),
        in_specs=[pl.BlockSpec((nb, d), lambda j: (0, 0)),
                  pl.BlockSpec((d, d), lambda j: (0, j)),
                  pl.BlockSpec((1, d), lambda j: (0, j))],
        out_specs=pl.BlockSpec((nb, d), lambda j: (0, j)),
        out_shape=jax.ShapeDtypeStruct((nb, n_out), F32),
        compiler_params=_compiler_params(1),
        name="adaln_modulation",
    )(c, w_ada, b_ada.reshape(1, n_out))


def _lambda_kernel(l_ref, o_ref, *, lam_init):
    a = jnp.sum(l_ref[0:1, :] * l_ref[1:2, :], axis=-1, keepdims=True)
    b = jnp.sum(l_ref[2:3, :] * l_ref[3:4, :], axis=-1, keepdims=True)
    lam = jnp.exp(a) - jnp.exp(b) + lam_init
    o_ref[...] = jnp.broadcast_to(lam, o_ref.shape)


def _lambda(lq1, lk1, lq2, lk2, lam_init):
    stacked = jnp.stack([lq1, lk1, lq2, lk2]).astype(F32)
    return pl.pallas_call(
        functools.partial(_lambda_kernel, lam_init=lam_init),
        out_shape=jax.ShapeDtypeStruct((1, LANES), F32),
        name="diff_lambda",
    )(stacked)


def _rope_tables(seq):
    half = ROT_DIM // 2
    inv_freq = ROPE_THETA ** (-jnp.arange(0, ROT_DIM, 2, dtype=F32) / ROT_DIM)
    ang = jnp.arange(seq, dtype=F32)[:, None] * inv_freq[None, :]
    cos, sin = jnp.cos(ang), jnp.sin(ang)
    ones = jnp.ones((seq, HEAD_DIM - ROT_DIM), F32)
    zeros = jnp.zeros((seq, HEAD_DIM - ROT_DIM), F32)
    zh = jnp.zeros((seq, half), F32)
    cos_t = jnp.concatenate([cos, cos, ones], axis=1)
    sin_a = jnp.concatenate([-sin, zh, zeros], axis=1)
    sin_b = jnp.concatenate([zh, sin, zeros], axis=1)
    rep = LANES // HEAD_DIM
    return tuple(jnp.tile(t, (1, rep)) for t in (cos_t, sin_a, sin_b))


def _inproj_kernel(x_ref, xnext_ref, mod_ref, w_ref, wvt_ref, bg_ref, cos_ref, sa_ref, sb_ref,
                   q_ref, k_ref, vt_ref, u_ref, g_ref, hbuf, *, d, qk_scale):
    half = ROT_DIM // 2
    i = pl.program_id(1)
    sh1 = mod_ref[0, 0:1, :]
    sc1 = mod_ref[0, 1:2, :]

    def modnorm(t):
        return (_norm(t) * (1.0 + sc1) + sh1).astype(BF16)

    @pl.when(i == 0)
    def _():
        hbuf[0] = modnorm(x_ref[0])

    h = hbuf[lax.rem(i, 2)]
    hbuf[lax.rem(i + 1, 2)] = modnorm(xnext_ref[0])

    def proj(c0, width):
        return jnp.dot(h, w_ref[:, c0:c0 + width], preferred_element_type=F32)

    cos_t, sin_a, sin_b = cos_ref[...], sa_ref[...], sb_ref[...]

    def store_rope(t, o_ref, scale):
        for j in range(d // HEAD_WIDTH):
            s = t[:, j * HEAD_WIDTH:(j + 1) * HEAD_WIDTH]
            r = (s * cos_t + pltpu.roll(s, LANES - half, 1) * sin_a
                 + pltpu.roll(s, half, 1) * sin_b)
            if scale != 1.0:
                r = r * scale
            o_ref[0, j] = r.astype(BF16)

    store_rope(proj(0, d), q_ref, qk_scale)
    store_rope(proj(d, d), k_ref, 1.0)
    v_t = lax.dot_general(wvt_ref[...], h, (((1,), (1,)), ((), ())),
                          preferred_element_type=F32).astype(BF16)
    n_heads = d // HEAD_WIDTH
    vt_ref[0, :, 0:HEAD_WIDTH, :] = v_t.reshape(n_heads, HEAD_WIDTH, v_t.shape[1])
    vt_ref[0, :, HEAD_WIDTH:, :] = jnp.ones((n_heads, SUM_ROWS, v_t.shape[1]), BF16)
    glu_a = proj(3 * d, d)
    glu_b = proj(4 * d, d)
    u_ref[0] = (glu_a * jax.nn.sigmoid(glu_b)).reshape(u_ref.shape[1:])
    gate = proj(5 * d, 2 * d) + bg_ref[...]
    g_ref[0] = jax.nn.sigmoid(gate).astype(BF16)


def _in_projection(x, mod3, w_in, w_v_t, b_gate, tables, tm):
    nb, seq, d = x.shape
    n_in = w_in.shape[1]
    n_heads = d // HEAD_WIDTH
    assert HEAD_WIDTH == LANES
    tok = lambda b, i: (b, i, 0)
    tab = lambda b, i: (i, 0)
    act = lambda width: jax.ShapeDtypeStruct((nb, seq, width), BF16)
    tok_spec = pl.BlockSpec((1, tm, d), tok)
    head_spec = pl.BlockSpec((1, n_heads, tm, HEAD_WIDTH), lambda b, i: (b, 0, i, 0))
    head_major = jax.ShapeDtypeStruct((nb, n_heads, seq, HEAD_WIDTH), BF16)
    return pl.pallas_call(
        functools.partial(_inproj_kernel, d=d, qk_scale=HEAD_DIM ** -0.5 * math.log2(math.e)),
        grid=(nb, seq // tm),
        in_specs=[tok_spec,
                  pl.BlockSpec((1, tm, d), lambda b, i: (b, jnp.minimum(i + 1, seq // tm - 1), 0)),
                  pl.BlockSpec((1,) + mod3.shape[1:], lambda b, i: (b, 0, 0)),
                  _resident((d, n_in)),
                  _resident((d, d)),
                  _resident((1, 2 * d)),
                  pl.BlockSpec((tm, LANES), tab),
                  pl.BlockSpec((tm, LANES), tab),
                  pl.BlockSpec((tm, LANES), tab)],
        out_specs=[head_spec, head_spec,
                   pl.BlockSpec((1, n_heads, V_ROWS, tm), lambda b, i: (b, 0, 0, i)),
                   pl.BlockSpec((1, tm, d // LANES, LANES), lambda b, i: (b, i, 0, 0)),
                   pl.BlockSpec((1, tm, 2 * d), tok)],
        out_shape=[head_major, head_major,
                   jax.ShapeDtypeStruct((nb, n_heads, V_ROWS, seq), BF16),
                   jax.ShapeDtypeStruct((nb, seq, d // LANES, LANES), F32),
                   act(2 * d)],
        scratch_shapes=[pltpu.VMEM((2, tm, d), BF16)],
        compiler_params=_compiler_params(2),
        name="in_projection",
    )(x, x, mod3, w_in, w_v_t, b_gate.reshape(1, 2 * d), *tables)


ATTN_KEY_CHUNK = 512
ATTN_KEY_ROWS = 16


def _attn_kernel(q_ref, k_ref, vt_ref, lam_ref, g_ref, o_ref,
                 s_a, s_b, p_a, p_b, m_a, m_b, *, tq, out_scale):
    heads, seq = k_ref.shape[1], k_ref.shape[2]
    n_blk = seq // tq
    n_items = heads * n_blk
    key_chunk = min(ATTN_KEY_CHUNK, seq)

    def locate(e):
        if isinstance(e, int):
            return e // n_blk, (e % n_blk) * tq
        return lax.div(e, n_blk), pl.multiple_of(lax.rem(e, n_blk) * tq, tq)

    def fold_rows(t, op):
        acc = t[0:SUBLANES]
        for r0 in range(SUBLANES, t.shape[0], SUBLANES):
            acc = op(acc, t[r0:r0 + SUBLANES])
        return acc

    def all_rows(t, reduce):
        return jnp.broadcast_to(reduce(t, axis=0, keepdims=True), t.shape)

    def stage(score_job, softmax_job, value_job):
        if score_job is not None:
            head, row0 = locate(score_job[0])
            q = q_ref[0, head, pl.ds(row0, tq), :]
            lane = lax.broadcasted_iota(jnp.int32, q.shape, 1)
            zero = jnp.zeros_like(q)
            qq = jnp.concatenate([jnp.where(lane < HEAD_DIM, q, zero),
                                  jnp.where(lane >= HEAD_DIM, q, zero)], axis=0)
        if softmax_job is not None:
            m = softmax_job[1][...]
        m_acc = None
        for c0 in range(0, seq, key_chunk):
            keys = slice(c0, c0 + key_chunk)
            if score_job is not None:
                st = lax.dot_general(k_ref[0, head, keys, :], qq, (((1,), (1,)), ((), ())),
                                     preferred_element_type=F32)
                score_job[1][keys, :] = st
                m_c = fold_rows(st, jnp.maximum)
                m_acc = m_c if m_acc is None else jnp.maximum(m_acc, m_c)
            if softmax_job is not None:
                s_buf, _, p_buf = softmax_job
                for r0 in range(c0, c0 + key_chunk, ATTN_KEY_ROWS):
                    parts = []
                    for r in range(r0, r0 + ATTN_KEY_ROWS, SUBLANES):
                        parts.append(jnp.exp2(s_buf[r:r + SUBLANES, :] - m))
                    p_buf[r0:r0 + ATTN_KEY_ROWS, :] = jnp.concatenate(parts, axis=0).astype(BF16)
        if value_job is not None:
            v_head, v_row0 = locate(value_job[0])
            ot = jnp.dot(vt_ref[0, v_head], value_job[1][...], preferred_element_type=F32)
        if score_job is not None:
            score_job[2][...] = all_rows(m_acc, jnp.max)
        if value_job is not None:
            ot = ot[:HEAD_WIDTH] / ot[HEAD_WIDTH:HEAD_WIDTH + 1]
            lam = jnp.concatenate([lam_ref[...]] * (tq // LANES), axis=1)
            att = ot[:, :tq] - lam * ot[:, tq:]
            ms = jnp.mean(att * att, axis=0, keepdims=True)
            y = att * lax.rsqrt(ms + LN_EPS) * g_ref[...] * out_scale
            o_ref[0, v_head, pl.ds(v_row0, tq), :] = y.T.astype(BF16)

    buf_a = (s_a, m_a, p_a)
    buf_b = (s_b, m_b, p_b)
    stage((0, s_a, m_a), None, None)
    stage((1, s_b, m_b), buf_a, None)

    def pair(ee, carry):
        e = 2 * ee
        stage((e + 2, s_a, m_a), buf_b, (e, p_a))
        stage((e + 3, s_b, m_b), buf_a, (e + 1, p_b))
        return carry

    lax.fori_loop(0, n_items // 2 - 1, pair, 0)
    stage(None, buf_b, (n_items - 2, p_a))
    stage(None, None, (n_items - 1, p_b))


def _attention(q, k, v_t, lam, subln_g, lam_init, tq):
    nb, n_heads, seq, _ = q.shape
    heads = _heads_per_step(seq, tq, n_heads)
    assert (heads * (seq // tq)) % 2 == 0 and seq % tq == 0 and tq % LANES == 0
    assert seq % ATTN_KEY_ROWS == 0
    group = lambda b, g: (b, g, 0, 0)
    head_spec = pl.BlockSpec((1, heads, seq, HEAD_WIDTH), group)
    gain = jnp.broadcast_to(subln_g.astype(F32)[:, None], (HEAD_WIDTH, tq))
    score_buf = pltpu.VMEM((seq, 2 * tq), F32)
    prob_buf = pltpu.VMEM((seq, 2 * tq), BF16)
    stat_buf = pltpu.VMEM((SUBLANES, 2 * tq), F32)
    return pl.pallas_call(
        functools.partial(_attn_kernel, tq=tq, out_scale=1.0 - lam_init),
        grid=(nb, n_heads // heads),
        in_specs=[head_spec, head_spec,
                  pl.BlockSpec((1, heads, V_ROWS, seq), group),
                  pl.BlockSpec((1, LANES), lambda b, g: (0, 0)),
                  pl.BlockSpec((HEAD_WIDTH, tq), lambda b, g: (0, 0))],
        out_specs=head_spec,
        out_shape=jax.ShapeDtypeStruct((nb, n_heads, seq, HEAD_WIDTH), BF16),
        scratch_shapes=[score_buf, score_buf, prob_buf, prob_buf,
                        stat_buf, stat_buf],
        compiler_params=_compiler_params(2),
        name="diff_attention",
    )(q, k, v_t, lam, gain)


CONV_HALO = 16
CONV_ROWS = 32


def _mix_kernel(u_ref, up_ref, un_ref, att_ref, g_ref, x_ref, mod_ref,
                cw_ref, cb_ref, clg_ref, clb_ref, wpw_ref, bpw_ref, wo_ref,
                l1g_ref, l1b_ref, o_ref, ubuf, cbuf, *, alpha):
    tm, d = x_ref.shape[1], x_ref.shape[2]
    width = cw_ref.shape[0]
    pad = (width - 1) // 2
    i = pl.program_id(1)
    last = pl.num_programs(1) - 1

    ubuf[0:CONV_HALO] = jnp.where(i > 0, up_ref[0], 0.0)
    ubuf[CONV_HALO:CONV_HALO + tm] = u_ref[0]
    ubuf[CONV_HALO + tm:] = jnp.where(i < last, un_ref[0], 0.0)

    def conv_rows(rb, carry):
        r0 = rb * CONV_ROWS
        acc = jnp.zeros((CONV_ROWS, d // LANES, LANES), F32)
        for j in range(width):
            acc = acc + ubuf[pl.ds(r0 + (CONV_HALO - pad + j), CONV_ROWS)] * cw_ref[j]
        cbuf[pl.ds(r0, CONV_ROWS)] = acc
        return carry

    lax.fori_loop(0, tm // CONV_ROWS, conv_rows, 0, unroll=2)

    conv = cbuf[...].reshape(tm, d) + cb_ref[...]
    y = _norm(conv) * clg_ref[...] + clb_ref[...]
    y = y * jax.nn.sigmoid(y)
    y_b = jnp.dot(y.astype(BF16), wpw_ref[...], preferred_element_type=F32) + bpw_ref[...]
    gates = g_ref[0]
    att = jnp.concatenate([att_ref[0, j] for j in range(att_ref.shape[1])], axis=1)
    merged = gates[:, :d] * att + gates[:, d:] * y_b
    mix = jnp.dot(merged.astype(BF16), wo_ref[...], preferred_element_type=F32)
    g1 = mod_ref[0, 2:3, :]
    z = alpha * x_ref[0] + g1 * mix
    o_ref[0] = _norm(z) * l1g_ref[...] + l1b_ref[...]


def _mixer_output(u, att, gates, x, mod3, conv_w, conv_b, conv_ln_g, conv_ln_b,
                  w_pw2, b_pw2, w_o, ln1_g, ln1_b, alpha, tm):
    nb, seq, d = x.shape
    width = conv_w.shape[0]
    n_heads = att.shape[1]
    assert (width - 1) // 2 < CONV_HALO and tm % CONV_ROWS == 0 and tm % CONV_HALO == 0
    hb = tm // CONV_HALO
    n_halo = seq // CONV_HALO
    rows3 = (d // LANES, LANES)
    tok = lambda b, i: (b, i, 0)
    row = lambda v: v.reshape(1, d).astype(F32)
    vec = pl.BlockSpec((1, d), lambda b, i: (0, 0))
    return pl.pallas_call(
        functools.partial(_mix_kernel, alpha=alpha),
        grid=(nb, seq // tm),
        in_specs=[pl.BlockSpec((1, tm) + rows3, lambda b, i: (b, i, 0, 0)),
                  pl.BlockSpec((1, CONV_HALO) + rows3,
                               lambda b, i: (b, jnp.maximum(i * hb - 1, 0), 0, 0)),
                  pl.BlockSpec((1, CONV_HALO) + rows3,
                               lambda b, i: (b, jnp.minimum((i + 1) * hb, n_halo - 1), 0, 0)),
                  pl.BlockSpec((1, n_heads, tm, HEAD_WIDTH), lambda b, i: (b, 0, i, 0)),
                  pl.BlockSpec((1, tm, 2 * d), tok),
                  pl.BlockSpec((1, tm, d), tok),
                  pl.BlockSpec((1,) + mod3.shape[1:], lambda b, i: (b, 0, 0)),
                  _resident((width, d // LANES, LANES)), vec, vec, vec,
                  _resident((d, d)), vec,
                  _resident((d, d)), vec, vec],
        out_specs=pl.BlockSpec((1, tm, d), tok),
        out_shape=jax.ShapeDtypeStruct((nb, seq, d), F32),
        scratch_shapes=[pltpu.VMEM((tm + 2 * CONV_HALO, d // LANES, LANES), F32),
                        pltpu.VMEM((tm, d // LANES, LANES), F32)],
        compiler_params=_compiler_params(2),
        name="mixer_output",
    )(u, u, u, att, gates, x, mod3, conv_w.astype(F32).reshape(width, d // LANES, LANES),
      row(conv_b), row(conv_ln_g),
      row(conv_ln_b), w_pw2, row(b_pw2), w_o, row(ln1_g), row(ln1_b))


FFN_HALO = 8
FFN_CHUNK = 256
GELU_FOLD = 2.0 ** -0.5


def _ffn_kernel(x_ref, xp_ref, xn_ref, mod_ref, wup_ref, fcw_ref, fcb_ref, wdn_ref,
                l2g_ref, l2b_ref, o_ref, gbuf, vbuf, abuf, *, alpha, d_ff):
    tm = x_ref.shape[1]
    i = pl.program_id(1)
    last = pl.num_programs(1) - 1
    sh2 = mod_ref[0, 3:4, :]
    sc2 = mod_ref[0, 4:5, :]
    g2 = mod_ref[0, 5:6, :]

    def modnorm(t):
        return _norm(t) * (1.0 + sc2) + sh2

    x = x_ref[0]
    h_prev = jnp.where(i > 0, modnorm(xp_ref[0]), 0.0)
    h_next = jnp.where(i < last, modnorm(xn_ref[0]), 0.0)
    h_ext = jnp.concatenate([h_prev, modnorm(x), h_next], axis=0).astype(BF16)

    def conv3(buf, c0):
        cols = slice(c0, c0 + FFN_CHUNK)
        up = buf[...]
        rows = up.shape[0]
        main = slice(FFN_HALO, FFN_HALO + tm)
        return (pltpu.roll(up, 1, 0)[main] * fcw_ref[0:1, cols]
                + up[main] * fcw_ref[1:2, cols]
                + pltpu.roll(up, rows - 1, 0)[main] * fcw_ref[2:3, cols]
                + fcb_ref[:, cols])

    for c in range(d_ff // FFN_CHUNK):
        c0 = c * FFN_CHUNK
        gbuf[...] = jnp.dot(h_ext, wup_ref[:, c0:c0 + FFN_CHUNK], preferred_element_type=F32)
        vbuf[...] = jnp.dot(h_ext, wup_ref[:, d_ff + c0:d_ff + c0 + FFN_CHUNK],
                            preferred_element_type=F32)
        fg = conv3(gbuf, c0)
        fv = conv3(vbuf, d_ff + c0)
        abuf[:, c0:c0 + FFN_CHUNK] = (fg * fv * (1.0 + lax.erf(fg))).astype(BF16)

    f = jnp.dot(abuf[...], wdn_ref[...], preferred_element_type=F32)
    z = alpha * x + g2 * f
    o_ref[0] = _norm(z) * l2g_ref[...] + l2b_ref[...]


def _channel_mixer(x1, mod3, w_up, ffn_conv_w, ffn_conv_b, w_down, ln2_g, ln2_b, alpha, tm):
    nb, seq, d = x1.shape
    d_ff = w_down.shape[0]
    assert d_ff % FFN_CHUNK == 0 and ffn_conv_w.shape[0] == 3 and tm % FFN_HALO == 0
    hb = tm // FFN_HALO
    n_halo = seq // FFN_HALO
    tok = lambda b, i: (b, i, 0)
    vec = pl.BlockSpec((1, d), lambda b, i: (0, 0))
    return pl.pallas_call(
        functools.partial(_ffn_kernel, alpha=alpha, d_ff=d_ff),
        grid=(nb, seq // tm),
        in_specs=[pl.BlockSpec((1, tm, d), tok),
                  pl.BlockSpec((1, FFN_HALO, d), lambda b, i: (b, jnp.maximum(i * hb - 1, 0), 0)),
                  pl.BlockSpec((1, FFN_HALO, d),
                               lambda b, i: (b, jnp.minimum((i + 1) * hb, n_halo - 1), 0)),
                  pl.BlockSpec((1,) + mod3.shape[1:], lambda b, i: (b, 0, 0)),
                  _resident((d, 2 * d_ff)),
                  _resident((3, 2 * d_ff)),
                  _resident((1, 2 * d_ff)),
                  _resident((d_ff, d)), vec, vec],
        out_specs=pl.BlockSpec((1, tm, d), tok),
        out_shape=jax.ShapeDtypeStruct((nb, seq, d), F32),
        scratch_shapes=[pltpu.VMEM((tm + 2 * FFN_HALO, FFN_CHUNK), F32),
                        pltpu.VMEM((tm + 2 * FFN_HALO, FFN_CHUNK), F32),
                        pltpu.VMEM((tm, d_ff), BF16)],
        compiler_params=_compiler_params(2),
        name="channel_mixer",
    )(x1, x1, x1, mod3, w_up, ffn_conv_w.astype(F32) * GELU_FOLD,
      ffn_conv_b.reshape(1, 2 * d_ff).astype(F32) * GELU_FOLD,
      w_down, ln2_g.reshape(1, d).astype(F32), ln2_b.reshape(1, d).astype(F32))


def _encoder_layer(x, mod3, lam, lam_init, alpha, w_in, w_v_t, b_gate, subln_g, conv_w, conv_b,
                   conv_ln_g, conv_ln_b, w_pw2, b_pw2, w_o, ln1_g, ln1_b, w_up, ffn_conv_w,
                   ffn_conv_b, w_down, ln2_g, ln2_b):
    seq = x.shape[1]
    tm, tq = _tiles(seq)
    q, k, v_t, u, gates = _in_projection(x, mod3, w_in, w_v_t, b_gate, _rope_tables(seq), tm)
    att = _attention(q, k, v_t, lam, subln_g, lam_init, tq)
    x1 = _mixer_output(u, att, gates, x, mod3, conv_w, conv_b, conv_ln_g, conv_ln_b,
                       w_pw2, b_pw2, w_o, ln1_g, ln1_b, alpha, tm)
    return _channel_mixer(x1, mod3, w_up, ffn_conv_w, ffn_conv_b, w_down, ln2_g, ln2_b, alpha, tm)


def kernel(x_prompt, x_sample, c_prompt, c_sample, w_ada, b_ada, w_in, b_gate, lambda_q1, lambda_k1, lambda_q2, lambda_k2, subln_g, conv_w, conv_b, conv_ln_g, conv_ln_b, w_pw2, b_pw2, w_o, ln1_g, ln1_b, w_up, ffn_conv_w, ffn_conv_b, w_down, ln2_g, ln2_b):
    depth = w_ada.shape[0]
    d = x_prompt.shape[-1]
    alpha = (2.0 * depth) ** 0.25
    n_prompt = c_prompt.shape[0]
    y_prompt, y_sample = x_prompt, x_sample
    for l in range(depth):
        lam_init = 0.8 - 0.6 * math.exp(-0.3 * l)
        mod = _modulation(jnp.concatenate([c_prompt, c_sample], axis=0), w_ada[l], b_ada[l])
        mod3 = mod.reshape(mod.shape[0], -1, d)
        lam = _lambda(lambda_q1[l], lambda_k1[l], lambda_q2[l], lambda_k2[l], lam_init)
        w_v_t = w_in[l][:, 2 * d:3 * d].T.astype(BF16)
        weights = (w_in[l].astype(BF16), w_v_t, b_gate[l], subln_g[l], conv_w[l], conv_b[l],
                   conv_ln_g[l], conv_ln_b[l], w_pw2[l].astype(BF16), b_pw2[l],
                   w_o[l].astype(BF16), ln1_g[l], ln1_b[l], w_up[l].astype(BF16),
                   ffn_conv_w[l], ffn_conv_b[l], w_down[l].astype(BF16), ln2_g[l], ln2_b[l])
        y_prompt = _encoder_layer(y_prompt, mod3[:n_prompt], lam, lam_init, alpha, *weights)
        y_sample = _encoder_layer(y_sample, mod3[n_prompt:], lam, lam_init, alpha, *weights)
    return (y_prompt, y_sample)
```
